```python
import math
import jax, jax.numpy as jnp
from jax import lax
import numpy as np

D_MODEL = 1024
BATCH = 8
SEQ = 2048
DEPTH = 2

HEAD_DIM = 64
BLOCK_Q = 128
N_SB_HEADS = 4
DIL_PATTERNS = ((128, 1), (512, 4), (2048, 16))
N_DIL_GROUPS = len(DIL_PATTERNS)
N_DIL_HEADS = 4
N_FOX_HEADS = 4
N_DIFF_HEADS = 4
DIFF_QK_DIM = HEAD_DIM // 2
N_BRANCH = 4
BRANCH_WIDTH = 4 * HEAD_DIM
D_FF = 4 * D_MODEL
RMS_EPS = 1e-6
N_ALIBI = N_DIL_GROUPS * N_DIL_HEADS + N_DIFF_HEADS

SB_W = N_SB_HEADS * HEAD_DIM
DIL_W = N_DIL_GROUPS * N_DIL_HEADS * HEAD_DIM
FOX_W = N_FOX_HEADS * HEAD_DIM
DIFF_QK_W = N_DIFF_HEADS * 2 * DIFF_QK_DIM
DIFF_V_W = N_DIFF_HEADS * HEAD_DIM
GATE_W = N_BRANCH * D_MODEL
SPLIT_SIZES = (SB_W,) * 3 + (DIL_W,) * 3 + (FOX_W,) * 3 + (N_FOX_HEADS,) + (DIFF_QK_W, DIFF_QK_W, DIFF_V_W) + (GATE_W,)
D_IN = sum(SPLIT_SIZES)
SPLIT_OFFSETS = tuple(sum(SPLIT_SIZES[:i + 1]) for i in range(len(SPLIT_SIZES) - 1))

kernel_name = 'hybrid_gated_four_mixer_decoder'


def rms_norm(x, g):
    xf = x.astype(jnp.float32)
    y = xf * lax.rsqrt(jnp.mean(xf * xf, axis=-1, keepdims=True) + RMS_EPS)
    return (y * g.astype(jnp.float32)).astype(x.dtype)


def alibi_slopes():
    return jnp.asarray(2.0 ** (-8.0 * np.arange(1, N_ALIBI + 1) / N_ALIBI), dtype=jnp.float32)


def _stick_breaking(q_blk, k, v, dist):
    z = jnp.einsum('bqhd,bshd->bhqs', q_blk, k).astype(jnp.float32) * (HEAD_DIM ** -0.5)
    strict = dist > 0
    log_keep = jnp.where(strict, jax.nn.log_sigmoid(-z), 0.0)
    log_between = lax.cumsum(log_keep, axis=3, reverse=True) - log_keep
    a = jnp.where(strict, jnp.exp(jax.nn.log_sigmoid(z) + log_between), 0.0)
    return jnp.einsum('bhqs,bshd->bqhd', a.astype(v.dtype), v)


def _dilated(q_blk, k, v, t_idx, slopes):
    outs, lses = [], []
    for g, (window, dil) in enumerate(DIL_PATTERNS):
        dists = dil * jnp.arange(window // dil + 1)
        idx = t_idx[:, None] - dists[None, :]
        valid = idx >= 0
        idx = jnp.maximum(idx, 0)
        kg = jnp.take(k[:, :, g], idx, axis=1)
        vg = jnp.take(v[:, :, g], idx, axis=1)
        s = jnp.einsum('bqhd,bqnhd->bhqn', q_blk[:, :, g], kg).astype(jnp.float32) * (HEAD_DIM ** -0.5)
        s = s - slopes[g][:, None, None] * dists.astype(jnp.float32)
        s = jnp.where(valid, s, -jnp.inf)
        lse = jax.nn.logsumexp(s, axis=-1)
        p = jnp.exp(s - lse[..., None])
        outs.append(jnp.einsum('bhqn,bqnhd->bqhd', p.astype(v.dtype), vg))
        lses.append(lse)
    alpha = jax.nn.softmax(jnp.stack(lses, axis=0), axis=0)
    alpha = jnp.transpose(alpha, (0, 1, 3, 2))[..., None]
    return jnp.sum(alpha.astype(v.dtype) * jnp.stack(outs, axis=0), axis=0)


def _forgetting(q_blk, k, v, cum_blk, cum, dist):
    s = jnp.einsum('bqhd,bshd->bhqs', q_blk, k).astype(jnp.float32) * (HEAD_DIM ** -0.5)
    s = s + jnp.transpose(cum_blk, (0, 2, 1))[..., None] - jnp.transpose(cum, (0, 2, 1))[:, :, None, :]
    p = jax.nn.softmax(jnp.where(dist >= 0, s, -jnp.inf), axis=-1)
    return jnp.einsum('bhqs,bshd->bqhd', p.astype(v.dtype), v)


def _differential(q_blk, k, v, dist, slopes, lam):
    s = jnp.einsum('bqhcd,bshcd->bchqs', q_blk, k).astype(jnp.float32) * (DIFF_QK_DIM ** -0.5)
    s = s - slopes[:, None, None] * dist.astype(jnp.float32)
    p = jax.nn.softmax(jnp.where(dist >= 0, s, -jnp.inf), axis=-1)
    w = p[:, 0] - lam * p[:, 1]
    return jnp.einsum('bhqs,bshd->bqhd', w.astype(v.dtype), v)


def _token_mixers(h, w_in, b_forget, lam, diff_g, diff_out_scale, slopes_dil, slopes_diff):
    bsz, seq, _ = h.shape
    proj = h @ w_in
    (qa, ka, va, qb, kb, vb, qc, kc, vc, f_logit, qd, kd, vd, gate_logit) = jnp.split(proj, SPLIT_OFFSETS, axis=-1)
    qa, ka, va = (a.reshape(bsz, seq, N_SB_HEADS, HEAD_DIM) for a in (qa, ka, va))
    qb, kb, vb = (a.reshape(bsz, seq, N_DIL_GROUPS, N_DIL_HEADS, HEAD_DIM) for a in (qb, kb, vb))
    qc, kc, vc = (a.reshape(bsz, seq, N_FOX_HEADS, HEAD_DIM) for a in (qc, kc, vc))
    qd, kd = (a.reshape(bsz, seq, N_DIFF_HEADS, 2, DIFF_QK_DIM) for a in (qd, kd))
    vd = vd.reshape(bsz, seq, N_DIFF_HEADS, HEAD_DIM)
    log_f = jax.nn.log_sigmoid(f_logit.astype(jnp.float32) + b_forget.astype(jnp.float32))
    cum = jnp.cumsum(log_f, axis=1)
    s_idx = jnp.arange(seq)

    def block(q0):
        t_idx = q0 + jnp.arange(BLOCK_Q)
        dist = t_idx[:, None] - s_idx[None, :]
        sl = lambda a: lax.dynamic_slice_in_dim(a, q0, BLOCK_Q, axis=1)
        return (_stick_breaking(sl(qa), ka, va, dist),
                _dilated(sl(qb), kb, vb, t_idx, slopes_dil),
                _forgetting(sl(qc), kc, vc, sl(cum), cum, dist),
                _differential(sl(qd), kd, vd, dist, slopes_diff, lam))

    o_a, o_b, o_c, o_d = lax.map(block, jnp.arange(seq // BLOCK_Q) * BLOCK_Q)
    unblock = lambda o: jnp.moveaxis(o, 0, 1).reshape(bsz, seq, o.shape[3], o.shape[4])
    o_d = rms_norm(unblock(o_d), diff_g) * diff_out_scale
    ys = jnp.stack([unblock(o_a), unblock(o_b), unblock(o_c), o_d], axis=2)
    return ys.reshape(bsz, seq, N_BRANCH, BRANCH_WIDTH), gate_logit


def setup_inputs(seed: int = 0) -> dict:
    key = jax.random.key(seed)
    ks = jax.random.split(key, 16)
    nrm = lambda k, shape, scale: jax.random.normal(k, shape, jnp.float32) * scale
    return {
        'x': nrm(ks[0], (BATCH, SEQ, D_MODEL), 1.0),
        'mix_norm_g': 1.0 + nrm(ks[1], (DEPTH, D_MODEL), 0.05),
        'w_in': nrm(ks[2], (DEPTH, D_MODEL, D_IN), D_MODEL ** -0.5),
        'b_forget': jax.random.uniform(ks[3], (DEPTH, N_FOX_HEADS), jnp.float32, 1.0, 4.0),
        'lambda_q1': nrm(ks[4], (DEPTH, DIFF_QK_DIM), 0.1),
        'lambda_k1': nrm(ks[5], (DEPTH, DIFF_QK_DIM), 0.1),
        'lambda_q2': nrm(ks[6], (DEPTH, DIFF_QK_DIM), 0.1),
        'lambda_k2': nrm(ks[7], (DEPTH, DIFF_QK_DIM), 0.1),
        'diff_norm_g': 1.0 + nrm(ks[8], (DEPTH, HEAD_DIM), 0.05),
        'w_branch': nrm(ks[9], (DEPTH, N_BRANCH, BRANCH_WIDTH, D_MODEL), BRANCH_WIDTH ** -0.5),
        'w_out': nrm(ks[10], (DEPTH, D_MODEL, D_MODEL), D_MODEL ** -0.5),
        'mlp_norm_g': 1.0 + nrm(ks[11], (DEPTH, D_MODEL), 0.05),
        'w_up': nrm(ks[12], (DEPTH, D_MODEL, D_FF), D_MODEL ** -0.5),
        'w_down': nrm(ks[13], (DEPTH, D_FF, D_MODEL), D_FF ** -0.5),
        'final_norm_g': 1.0 + nrm(ks[14], (D_MODEL,), 0.05),
    }


def reference(x, mix_norm_g, w_in, b_forget, lambda_q1, lambda_k1, lambda_q2, lambda_k2, diff_norm_g, w_branch, w_out, mlp_norm_g, w_up, w_down, final_norm_g):
    bsz, seq, _ = x.shape
    slopes = alibi_slopes()
    n0, n1 = N_DIL_HEADS, N_DIL_HEADS + N_DIFF_HEADS
    slopes_dil = jnp.stack([slopes[:n0], slopes[n1:n1 + N_DIL_HEADS], slopes[n1 + N_DIL_HEADS:]], axis=0)
    slopes_diff = slopes[n0:n1]
    for l in range(DEPTH):
        lambda_init = 0.8 - 0.6 * math.exp(-0.3 * l)
        lam = (jnp.exp(jnp.sum(lambda_q1[l].astype(jnp.float32) * lambda_k1[l].astype(jnp.float32)))
               - jnp.exp(jnp.sum(lambda_q2[l].astype(jnp.float32) * lambda_k2[l].astype(jnp.float32))) + lambda_init)
        h = rms_norm(x, mix_norm_g[l])
        ys, gate_logit = _token_mixers(h, w_in[l], b_forget[l], lam, diff_norm_g[l], 1.0 - lambda_init, slopes_dil, slopes_diff)
        branch = jnp.einsum('bsnc,ncd->bsnd', ys, w_branch[l])
        gates = jax.nn.sigmoid(gate_logit.reshape(bsz, seq, N_BRANCH, D_MODEL))
        x = x + jnp.sum(gates * branch, axis=2) @ w_out[l]
        h = rms_norm(x, mlp_norm_g[l])
        x = x + jnp.square(jax.nn.relu(h @ w_up[l])) @ w_down[l]
    return rms_norm(x, final_norm_g)
```

```python
import functools
import math

import numpy as np
import jax
import jax.numpy as jnp
from jax import lax
from jax.experimental import pallas as pl
from jax.experimental.pallas import tpu as pltpu

F32 = jnp.float32
BF16 = jnp.bfloat16

D_MODEL = 1024
HEAD_DIM = 64
N_HEADS = 4
DIL_PATTERNS = ((128, 1), (512, 4), (2048, 16))
N_GROUPS = len(DIL_PATTERNS)
DIFF_QK_DIM = HEAD_DIM // 2
D_FF = 4 * D_MODEL
RMS_EPS = 1e-6
N_ALIBI = N_GROUPS * N_HEADS + N_HEADS

LANES = 128
BW = N_HEADS * HEAD_DIM
NEG = -1e30
VMEM_LIMIT = 56 * 1024 * 1024

GATE_W = 4 * D_MODEL
OFF_QA, OFF_KA, OFF_VA = GATE_W, GATE_W + BW, GATE_W + 2 * BW
OFF_QB, OFF_KB, OFF_VB = GATE_W + 3 * BW, GATE_W + 6 * BW, GATE_W + 9 * BW
OFF_QC, OFF_KC, OFF_VC = GATE_W + 12 * BW, GATE_W + 13 * BW, GATE_W + 14 * BW
OFF_QD, OFF_KD, OFF_VD = GATE_W + 15 * BW, GATE_W + 16 * BW, GATE_W + 17 * BW
W_MAIN = GATE_W + 18 * BW
F_COL = 15 * BW


def _alibi_slopes():
    return 2.0 ** (-8.0 * np.arange(1, N_ALIBI + 1) / N_ALIBI)


def _cparams(sem):
    return pltpu.CompilerParams(dimension_semantics=sem, vmem_limit_bytes=VMEM_LIMIT)


def _rms(x, g):
    ms = jnp.mean(x * x, axis=-1, keepdims=True)
    return x * lax.rsqrt(ms + RMS_EPS) * g


def _split3(val):
    p1 = val.astype(BF16).astype(F32)
    r1 = val - p1
    p2 = r1.astype(BF16).astype(F32)
    p3 = (r1 - p2).astype(BF16).astype(F32)
    return p1, p2, p3


def _split2(val):
    hi = val.astype(BF16)
    lo = (val - hi.astype(F32)).astype(BF16)
    return hi, lo


def _dot_nt(a, b):
    return lax.dot_general(a, b, (((1,), (1,)), ((), ())), preferred_element_type=F32)


def _dot(a, b):
    return jnp.dot(a, b, preferred_element_type=F32)


def _fold_max(s):
    m = s[:, :LANES]
    for j in range(1, s.shape[1] // LANES):
        m = jnp.maximum(m, s[:, j * LANES:(j + 1) * LANES])
    return m


def _norm_proj_kernel(x_ref, g_ref, w_ref, cs_ref, o_ref, h_ref):
    @pl.when(pl.program_id(1) == 0)
    def _():
        h_ref[...] = _rms(x_ref[...], g_ref[...]).astype(BF16)

    acc = _dot(h_ref[...], w_ref[...])
    o_ref[...] = (acc * cs_ref[...]).astype(o_ref.dtype)


def _norm_proj(x2d, g, w, colscale, tm, tn):
    n, d = x2d.shape
    wcols = w.shape[1]
    return pl.pallas_call(
        _norm_proj_kernel,
        grid=(n // tm, wcols // tn),
        in_specs=[
            pl.BlockSpec((tm, d), lambda i, j: (i, 0)),
            pl.BlockSpec((1, d), lambda i, j: (0, 0)),
            pl.BlockSpec((d, tn), lambda i, j: (0, j)),
            pl.BlockSpec((1, tn), lambda i, j: (0, j)),
        ],
        out_specs=pl.BlockSpec((tm, tn), lambda i, j: (i, j)),
        out_shape=jax.ShapeDtypeStruct((n, wcols), BF16),
        scratch_shapes=[pltpu.VMEM((tm, d), BF16)],
        compiler_params=_cparams(("parallel", "arbitrary")),
        name="norm_proj",
    )(x2d, g, w, colscale)


def _forget_log_kernel(x_ref, g_ref, whi_ref, wlo_ref, b_ref, o_ref):
    h = _rms(x_ref[...], g_ref[...])
    h_hi, h_lo = _split2(h)
    f = _dot(h_hi, whi_ref[...]) + _dot(h_hi, wlo_ref[...]) + _dot(h_lo, whi_ref[...])
    y = f + b_ref[...]
    o_ref[...] = jnp.minimum(y, 0.0) - jnp.log(1.0 + jnp.exp(-jnp.abs(y)))


def _cumsum_kernel(l_ref, tril_ref, o_ref):
    l1, l2, l3 = _split3(l_ref[0])
    tril = tril_ref[...]
    o_ref[0] = (_dot(tril, l1.astype(BF16)) + _dot(tril, l2.astype(BF16))
                + _dot(tril, l3.astype(BF16)))


def _forget_cum(x2d, g, wf_hi, wf_lo, bf, tril, bsz, tm):
    n, d = x2d.shape
    s = n // bsz
    logf = pl.pallas_call(
        _forget_log_kernel,
        grid=(n // tm,),
        in_specs=[
            pl.BlockSpec((tm, d), lambda i: (i, 0)),
            pl.BlockSpec((1, d), lambda i: (0, 0)),
            pl.BlockSpec((d, LANES), lambda i: (0, 0)),
            pl.BlockSpec((d, LANES), lambda i: (0, 0)),
            pl.BlockSpec((1, LANES), lambda i: (0, 0)),
        ],
        out_specs=pl.BlockSpec((tm, LANES), lambda i: (i, 0)),
        out_shape=jax.ShapeDtypeStruct((n, LANES), F32),
        compiler_params=_cparams(("parallel",)),
        name="forget_log",
    )(x2d, g, wf_hi, wf_lo, bf)
    return pl.pallas_call(
        _cumsum_kernel,
        grid=(bsz,),
        in_specs=[
            pl.BlockSpec((1, s, LANES), lambda i: (i, 0, 0)),
            pl.BlockSpec((s, s), lambda i: (0, 0)),
        ],
        out_specs=pl.BlockSpec((1, s, LANES), lambda i: (i, 0, 0)),
        out_shape=jax.ShapeDtypeStruct((bsz, s, LANES), F32),
        compiler_params=_cparams(("parallel",)),
        name="forget_cumsum",
    )(logf.reshape(bsz, s, LANES), tril)


def _aug(base, keep, lane, lane0, pieces):
    out = jnp.where(keep, base, 0.0)
    for i, p in enumerate(pieces):
        out = jnp.where(lane == lane0 + i, p, out)
    return out


def _ones_at(lane, lane0, n):
    return jnp.where((lane >= lane0) & (lane < lane0 + n), 1.0, 0.0)


def _v_aug(v, lane, h):
    own = (lane < HEAD_DIM) if h == 0 else (lane >= HEAD_DIM)
    return jnp.where(own, v, 1.0).astype(BF16)


def _normalize_pair(acc0, acc1, lane):
    den0 = pltpu.roll(acc0, HEAD_DIM, 1)
    den1 = pltpu.roll(acc1, HEAD_DIM, 1)
    return jnp.where(lane < HEAD_DIM, acc0 / den0, acc1 / den1)


def _fox_kernel(q_ref, k_ref, v_ref, cum_ref, o_ref, ka_ref, va_ref, s_ref, *, tq):
    p = pl.program_id(1)
    i = pl.program_id(2)
    s_len = k_ref.shape[1]
    tk = tq

    @pl.when(i == 0)
    def _prep():
        lane = lax.broadcasted_iota(jnp.int32, (s_len, LANES), 1)
        k = k_ref[0].astype(F32)
        v = v_ref[0].astype(F32)
        cum = cum_ref[0]
        for h in range(2):
            own = (lane < HEAD_DIM) if h == 0 else (lane >= HEAD_DIM)
            spare = HEAD_DIM if h == 0 else 0
            c = jnp.sum(jnp.where(lane == 2 * p + h, cum, 0.0), axis=1, keepdims=True)
            pieces = _split3(jnp.broadcast_to(-c, (s_len, LANES)))
            ka_ref[h] = _aug(k, own, lane, spare, pieces).astype(BF16)
            va_ref[h] = _v_aug(v, lane, h)

    lane_q = lax.broadcasted_iota(jnp.int32, (tq, LANES), 1)
    q = q_ref[0].astype(F32)
    qa = []
    for h in range(2):
        own = (lane_q < HEAD_DIM) if h == 0 else (lane_q >= HEAD_DIM)
        spare = HEAD_DIM if h == 0 else 0
        qa.append(jnp.where(own, q, _ones_at(lane_q, spare, 3)).astype(BF16))

    def score(h, c):
        kc = ka_ref[h, pl.ds(pl.multiple_of(c * tk, tk), tk), :]
        return _dot_nt(qa[h], kc)

    row = lax.broadcasted_iota(jnp.int32, (tq, tk), 0)
    col = lax.broadcasted_iota(jnp.int32, (tq, tk), 1)
    causal = col <= row

    mcol = []
    for h in range(2):
        s = jnp.where(causal, score(h, i), NEG)
        s_ref[h, i] = s
        mcol.append(_fold_max(s))

    def body1(c, carry):
        out = []
        for h in range(2):
            s = score(h, c)
            s_ref[h, c] = s
            out.append(jnp.maximum(carry[h], _fold_max(s)))
        return tuple(out)

    mcol = lax.fori_loop(0, i, body1, tuple(mcol))

    accs = []
    for h in range(2):
        m = jnp.max(mcol[h], axis=1, keepdims=True)
        mb = jnp.broadcast_to(m, (tq, tk))

        def body2(c, acc, h=h, mb=mb):
            pr = jnp.exp(s_ref[h, c] - mb).astype(BF16)
            vc = va_ref[h, pl.ds(pl.multiple_of(c * tk, tk), tk), :]
            return acc + _dot(pr, vc)

        accs.append(lax.fori_loop(0, i + 1, body2, jnp.zeros((tq, LANES), F32)))

    o_ref[0] = _normalize_pair(accs[0], accs[1], lane_q).astype(o_ref.dtype)


def _fox_attn(proj, cum, tq):
    b, s, _ = proj.shape
    nq = s // tq
    qc, kc, vc = OFF_QC // LANES, OFF_KC // LANES, OFF_VC // LANES
    return pl.pallas_call(
        functools.partial(_fox_kernel, tq=tq),
        grid=(b, 2, nq),
        in_specs=[
            pl.BlockSpec((1, tq, LANES), lambda bi, p, i: (bi, i, qc + p)),
            pl.BlockSpec((1, s, LANES), lambda bi, p, i: (bi, 0, kc + p)),
            pl.BlockSpec((1, s, LANES), lambda bi, p, i: (bi, 0, vc + p)),
            pl.BlockSpec((1, s, LANES), lambda bi, p, i: (bi, 0, 0)),
        ],
        out_specs=pl.BlockSpec((1, tq, LANES), lambda bi, p, i: (bi, i, p)),
        out_shape=jax.ShapeDtypeStruct((b, s, BW), BF16),
        scratch_shapes=[
            pltpu.VMEM((2, s, LANES), BF16),
            pltpu.VMEM((2, s, LANES), BF16),
            pltpu.VMEM((2, nq, tq, tq), F32),
        ],
        compiler_params=_cparams(("parallel", "parallel", "arbitrary")),
        name="fox_attn",
    )(proj, proj, proj, cum)


def _diff_kernel(q_ref, k_ref, v_ref, lam_ref, g_ref, o_ref, ka_ref, va_ref, s_ref, *,
                 tq, slopes, out_scale, lambda_init):
    p = pl.program_id(1)
    i = pl.program_id(2)
    s_len = k_ref.shape[1]
    tk = tq
    dq = DIFF_QK_DIM

    @pl.when(i == 0)
    def _prep():
        lane = lax.broadcasted_iota(jnp.int32, (s_len, LANES), 1)
        pos = lax.broadcasted_iota(jnp.int32, (s_len, LANES), 0).astype(F32)
        k = k_ref[0].astype(F32)
        v = v_ref[0].astype(F32)
        for h in range(2):
            slope = jnp.where(p == 0, slopes[h], slopes[2 + h]).astype(F32)
            pieces = _split3(pos * slope)
            for c in range(2):
                m = 2 * h + c
                own = (lane >= m * dq) & (lane < (m + 1) * dq)
                spare = ((m + 1) % 4) * dq
                ka_ref[m] = _aug(k, own, lane, spare, pieces).astype(BF16)
            va_ref[h] = _v_aug(v, lane, h)

    lane_q = lax.broadcasted_iota(jnp.int32, (tq, LANES), 1)
    q = q_ref[0].astype(F32)
    qa = []
    for m in range(4):
        own = (lane_q >= m * dq) & (lane_q < (m + 1) * dq)
        spare = ((m + 1) % 4) * dq
        qa.append(jnp.where(own, q, _ones_at(lane_q, spare, 3)).astype(BF16))

    def score(m, c):
        kc = ka_ref[m, pl.ds(pl.multiple_of(c * tk, tk), tk), :]
        return _dot_nt(qa[m], kc)

    row = lax.broadcasted_iota(jnp.int32, (tq, tk), 0)
    col = lax.broadcasted_iota(jnp.int32, (tq, tk), 1)
    causal = col <= row

    mcol = []
    for m in range(4):
        s = jnp.where(causal, score(m, i), NEG)
        s_ref[m, i] = s
        mcol.append(_fold_max(s))

    def body1(c, carry):
        out = []
        for m in range(4):
            s = score(m, c)
            s_ref[m, c] = s
            out.append(jnp.maximum(carry[m], _fold_max(s)))
        return tuple(out)

    mcol = lax.fori_loop(0, i, body1, tuple(mcol))

    accs = []
    for m in range(4):
        h = m // 2
        mx = jnp.max(mcol[m], axis=1, keepdims=True)
        mb = jnp.broadcast_to(mx, (tq, tk))

        def body2(c, acc, m=m, h=h, mb=mb):
            pr = jnp.exp(s_ref[m, c] - mb).astype(BF16)
            vc = va_ref[h, pl.ds(pl.multiple_of(c * tk, tk), tk), :]
            return acc + _dot(pr, vc)

        accs.append(lax.fori_loop(0, i + 1, body2, jnp.zeros((tq, LANES), F32)))

    lp = lam_ref[...]
    lam = (jnp.exp(jnp.sum(lp[0:1] * lp[1:2], axis=1, keepdims=True))
           - jnp.exp(jnp.sum(lp[2:3] * lp[3:4], axis=1, keepdims=True)) + lambda_init)
    o1 = _normalize_pair(accs[0], accs[2], lane_q)
    o2 = _normalize_pair(accs[1], accs[3], lane_q)
    o = o1 - lam * o2
    r = lax.broadcasted_iota(jnp.int32, (LANES, LANES), 0) // HEAD_DIM
    cc = lax.broadcasted_iota(jnp.int32, (LANES, LANES), 1) // HEAD_DIM
    same = jnp.where(r == cc, 1.0, 0.0).astype(BF16)
    o2_hi, o2_lo = _split2(o * o)
    ms = (_dot(o2_hi, same) + _dot(o2_lo, same)) * (1.0 / HEAD_DIM)
    y = o * lax.rsqrt(ms + RMS_EPS) * g_ref[...]
    o_ref[0] = (y * out_scale).astype(o_ref.dtype)


def _diff_attn(proj, lam_params, g2, tq, slopes, out_scale, lambda_init):
    b, s, _ = proj.shape
    nq = s // tq
    qc, kc, vc = OFF_QD // LANES, OFF_KD // LANES, OFF_VD // LANES
    return pl.pallas_call(
        functools.partial(_diff_kernel, tq=tq, slopes=slopes, out_scale=out_scale,
                          lambda_init=lambda_init),
        grid=(b, 2, nq),
        in_specs=[
            pl.BlockSpec((1, tq, LANES), lambda bi, p, i: (bi, i, qc + p)),
            pl.BlockSpec((1, s, LANES), lambda bi, p, i: (bi, 0, kc + p)),
            pl.BlockSpec((1, s, LANES), lambda bi, p, i: (bi, 0, vc + p)),
            pl.BlockSpec((4, DIFF_QK_DIM), lambda bi, p, i: (0, 0)),
            pl.BlockSpec((1, LANES), lambda bi, p, i: (0, 0)),
        ],
        out_specs=pl.BlockSpec((1, tq, LANES), lambda bi, p, i: (bi, i, p)),
        out_shape=jax.ShapeDtypeStruct((b, s, BW), BF16),
        scratch_shapes=[
            pltpu.VMEM((4, s, LANES), BF16),
            pltpu.VMEM((2, s, LANES), BF16),
            pltpu.VMEM((4, nq, tq, tq), F32),
        ],
        compiler_params=_cparams(("parallel", "parallel", "arbitrary")),
        name="diff_attn",
    )(proj, proj, proj, lam_params, g2)


def _dil_plan(tq):
    t = np.arange(tq)[:, None]
    s = np.arange(tq)[None, :]
    masks, n_off = [], []
    for g, (window, dil) in enumerate(DIL_PATTERNS):
        n = 2 if g == N_GROUPS - 1 else (window + tq - 1) // tq + 1
        for d in range(n):
            dist = d * tq + t - s
            ok = (dist >= 0) & (dist <= window) & (dist % dil == 0)
            masks.append(np.where(ok, 0.0, NEG).astype(np.float32))
        n_off.append(n)
    return np.stack(masks), tuple(n_off)


def _dil_kernel(*refs, tq, slopes, n_off):
    q_refs, k_refs, v_refs = refs[0:3], refs[3:6], refs[6:9]
    mask_ref, o_ref, ka_ref, va_ref, s_ref = refs[9:]
    p = pl.program_id(1)
    i = pl.program_id(2)
    s_len = k_refs[0].shape[1]
    tk = tq
    g_last = N_GROUPS - 1

    def slope_of(g, h):
        return jnp.where(p == 0, slopes[g][h], slopes[g][2 + h]).astype(F32)

    @pl.when(i == 0)
    def _prep():
        lane = lax.broadcasted_iota(jnp.int32, (s_len, LANES), 1)
        pos = lax.broadcasted_iota(jnp.int32, (s_len, LANES), 0).astype(F32)
        for g in range(N_GROUPS):
            k = k_refs[g][0].astype(F32)
            v = v_refs[g][0].astype(F32)
            for h in range(2):
                own = (lane < HEAD_DIM) if h == 0 else (lane >= HEAD_DIM)
                spare = HEAD_DIM if h == 0 else 0
                pieces = _split3(pos * slope_of(g, h))
                base = jnp.where(own, k, _ones_at(lane, spare + 3, 3))
                ka_ref[2 * g + h] = _aug(base, (lane < spare) | (lane >= spare + 3), lane,
                                         spare, pieces).astype(BF16)
                va_ref[2 * g + h] = _v_aug(v, lane, h)

    lane_q = lax.broadcasted_iota(jnp.int32, (tq, LANES), 1)
    tpos = (lax.broadcasted_iota(jnp.int32, (tq, LANES), 0) + i * tq).astype(F32)
    qa = []
    for g in range(N_GROUPS):
        q = q_refs[g][0].astype(F32)
        for h in range(2):
            own = (lane_q < HEAD_DIM) if h == 0 else (lane_q >= HEAD_DIM)
            spare = HEAD_DIM if h == 0 else 0
            pieces = _split3(-tpos * slope_of(g, h))
            base = jnp.where(own, q, _ones_at(lane_q, spare, 3))
            qa.append(_aug(base, (lane_q < spare + 3) | (lane_q >= spare + 6), lane_q,
                           spare + 3, pieces).astype(BF16))

    def score(gh, kb, mask_idx):
        kc = ka_ref[gh, pl.ds(pl.multiple_of(kb * tk, tk), tk), :]
        return _dot_nt(qa[gh], kc) + mask_ref[mask_idx]

    static = []
    mask_idx = 0
    for g in range(N_GROUPS):
        for d in range(n_off[g]):
            if g < g_last or d == 0:
                static.append((g, d, mask_idx))
            mask_idx += 1
    n_static = len(static)
    far_mask = mask_idx - 1

    mcol = [None, None]
    for slot, (g, d, mi) in enumerate(static):
        kb = jnp.maximum(i - d, 0)
        for h in range(2):
            s = score(2 * g + h, kb, mi)
            if d > 0:
                s = jnp.where(i >= d, s, NEG)
            s_ref[h, slot] = s
            fm = _fold_max(s)
            mcol[h] = fm if mcol[h] is None else jnp.maximum(mcol[h], fm)

    def body1(c, carry):
        out = []
        for h in range(2):
            s = score(2 * g_last + h, c, far_mask)
            s_ref[h, n_static + c] = s
            out.append(jnp.maximum(carry[h], _fold_max(s)))
        return tuple(out)

    mcol = lax.fori_loop(0, i, body1, tuple(mcol))

    accs = []
    for h in range(2):
        m = jnp.max(mcol[h], axis=1, keepdims=True)
        mb = jnp.broadcast_to(m, (tq, tk))
        acc = jnp.zeros((tq, LANES), F32)
        for slot, (g, d, mi) in enumerate(static):
            kb = jnp.maximum(i - d, 0)
            pr = jnp.exp(s_ref[h, slot] - mb).astype(BF16)
            vc = va_ref[2 * g + h, pl.ds(pl.multiple_of(kb * tk, tk), tk), :]
            acc = acc + _dot(pr, vc)

        def body2(c, acc, h=h, mb=mb):
            pr = jnp.exp(s_ref[h, n_static + c] - mb).astype(BF16)
            vc = va_ref[2 * g_last + h, pl.ds(pl.multiple_of(c * tk, tk), tk), :]
            return acc + _dot(pr, vc)

        accs.append(lax.fori_loop(0, i, body2, acc))

    o_ref[0] = _normalize_pair(accs[0], accs[1], lane_q).astype(o_ref.dtype)


def _dil_attn(proj, masks, n_off, tq, slopes):
    b, s, _ = proj.shape
    nq = s // tq
    n_static = sum(n_off[:-1]) + 1
    per_group = BW // LANES

    def spec(rows, off, g, is_q):
        blk = off // LANES + g * per_group
        if is_q:
            return pl.BlockSpec((1, rows, LANES), lambda bi, p, i: (bi, i, blk + p))
        return pl.BlockSpec((1, rows, LANES), lambda bi, p, i: (bi, 0, blk + p))

    in_specs = ([spec(tq, OFF_QB, g, True) for g in range(N_GROUPS)]
                + [spec(s, OFF_KB, g, False) for g in range(N_GROUPS)]
                + [spec(s, OFF_VB, g, False) for g in range(N_GROUPS)]
                + [pl.BlockSpec(masks.shape, lambda bi, p, i: (0, 0, 0))])
    return pl.pallas_call(
        functools.partial(_dil_kernel, tq=tq, slopes=slopes, n_off=n_off),
        grid=(b, 2, nq),
        in_specs=in_specs,
        out_specs=pl.BlockSpec((1, tq, LANES), lambda bi, p, i: (bi, i, p)),
        out_shape=jax.ShapeDtypeStruct((b, s, BW), BF16),
        scratch_shapes=[
            pltpu.VMEM((2 * N_GROUPS, s, LANES), BF16),
            pltpu.VMEM((2 * N_GROUPS, s, LANES), BF16),
            pltpu.VMEM((2, n_static + nq, tq, tq), F32),
        ],
        compiler_params=_cparams(("parallel", "parallel", "arbitrary")),
        name="dil_attn",
    )(*([proj] * 9), masks)


def _sb_kernel(q_ref, k_ref, v_ref, o_ref, *, tq):
    i = pl.program_id(2)
    tk = tq
    lane_q = lax.broadcasted_iota(jnp.int32, (tq, LANES), 1)
    q = q_ref[0].astype(F32)
    row = lax.broadcasted_iota(jnp.int32, (tq, tk), 0)
    col = lax.broadcasted_iota(jnp.int32, (tq, tk), 1)
    strict = col < row
    tri = jnp.where(row >= col, 1.0, 0.0).astype(BF16)
    ones = jnp.ones((tk, LANES), BF16)

    def step(h, qh, c, run, acc, diag):
        kc = k_ref[0, pl.ds(pl.multiple_of(c * tk, tk), tk), :]
        vc = v_ref[0, pl.ds(pl.multiple_of(c * tk, tk), tk), :]
        z = _dot_nt(qh, kc)
        lk = -(jnp.maximum(z, 0.0) + jnp.log(1.0 + jnp.exp(-jnp.abs(z))))
        if diag:
            lk = jnp.where(strict, lk, 0.0)
        hi, lo = _split2(lk)
        suffix = _dot(hi, tri) + _dot(lo, tri)
        total = _dot(hi, ones) + _dot(lo, ones)
        run_b = jnp.concatenate([run] * (tk // LANES), axis=1)
        a = jnp.exp(z + suffix + run_b)
        if diag:
            a = jnp.where(strict, a, 0.0)
        acc = acc + _dot(a.astype(BF16), vc)
        return run + total, acc

    outs = []
    for h in range(2):
        own = (lane_q < HEAD_DIM) if h == 0 else (lane_q >= HEAD_DIM)
        qh = jnp.where(own, q, 0.0).astype(BF16)
        run, acc = step(h, qh, i, jnp.zeros((tq, LANES), F32), jnp.zeros((tq, LANES), F32), True)

        def body(n, carry, h=h, qh=qh):
            return step(h, qh, i - 1 - n, carry[0], carry[1], False)

        run, acc = lax.fori_loop(0, i, body, (run, acc))
        outs.append(acc)
    o_ref[0] = jnp.where(lane_q < HEAD_DIM, outs[0], outs[1]).astype(o_ref.dtype)


def _sb_attn(proj, tq):
    b, s, _ = proj.shape
    nq = s // tq
    qc, kc, vc = OFF_QA // LANES, OFF_KA // LANES, OFF_VA // LANES
    return pl.pallas_call(
        functools.partial(_sb_kernel, tq=tq),
        grid=(b, 2, nq),
        in_specs=[
            pl.BlockSpec((1, tq, LANES), lambda bi, p, i: (bi, i, qc + p)),
            pl.BlockSpec((1, s, LANES), lambda bi, p, i: (bi, 0, kc + p)),
            pl.BlockSpec((1, s, LANES), lambda bi, p, i: (bi, 0, vc + p)),
        ],
        out_specs=pl.BlockSpec((1, tq, LANES), lambda bi, p, i: (bi, i, p)),
        out_shape=jax.ShapeDtypeStruct((b, s, BW), BF16),
        compiler_params=_cparams(("parallel", "parallel", "arbitrary")),
        name="sb_attn",
    )(proj, proj, proj)


def _merge_kernel(x_ref, ya_ref, yb_ref, yc_ref, yd_ref, gate_ref, wb_ref, wo_ref, o_ref):
    merged = None
    for n, y_ref in enumerate((ya_ref, yb_ref, yc_ref, yd_ref)):
        br = _dot(y_ref[...], wb_ref[n])
        gl = gate_ref[:, n * D_MODEL:(n + 1) * D_MODEL].astype(F32)
        term = br / (1.0 + jnp.exp(-gl))
        merged = term if merged is None else merged + term
    o_ref[...] = x_ref[...] + _dot(merged.astype(BF16), wo_ref[...])


def _merge_out(x2d, ys, proj2d, wb, wo, tm):
    n, d = x2d.shape
    return pl.pallas_call(
        _merge_kernel,
        grid=(n // tm,),
        in_specs=[pl.BlockSpec((tm, d), lambda i: (i, 0))]
        + [pl.BlockSpec((tm, BW), lambda i: (i, 0))] * 4
        + [
            pl.BlockSpec((tm, GATE_W), lambda i: (i, 0)),
            pl.BlockSpec((4, BW, d), lambda i: (0, 0, 0)),
            pl.BlockSpec((d, d), lambda i: (0, 0)),
        ],
        out_specs=pl.BlockSpec((tm, d), lambda i: (i, 0)),
        out_shape=jax.ShapeDtypeStruct((n, d), F32),
        compiler_params=_cparams(("parallel",)),
        name="merge_out",
    )(x2d, *ys, proj2d, wb, wo)


def _mlp_kernel(x_ref, g_ref, wu_ref, wd_ref, gf_ref, o_ref, h_ref, acc_ref, *, final_norm):
    j = pl.program_id(1)

    @pl.when(j == 0)
    def _():
        h_ref[...] = _rms(x_ref[...], g_ref[...]).astype(BF16)

    up = _dot(h_ref[...], wu_ref[...])
    act = jnp.square(jnp.maximum(up, 0.0)).astype(BF16)
    part = _dot(act, wd_ref[...])

    @pl.when(j == 0)
    def _():
        acc_ref[...] = part

    @pl.when(j > 0)
    def _():
        acc_ref[...] += part

    @pl.when(j == pl.num_programs(1) - 1)
    def _():
        y = x_ref[...] + acc_ref[...]
        if final_norm:
            y = _rms(y, gf_ref[...])
        o_ref[...] = y


def _mlp(x2d, g, wu, wd, gf, tm, tf, final_norm):
    n, d = x2d.shape
    dff = wu.shape[1]
    return pl.pallas_call(
        functools.partial(_mlp_kernel, final_norm=final_norm),
        grid=(n // tm, dff // tf),
        in_specs=[
            pl.BlockSpec((tm, d), lambda i, j: (i, 0)),
            pl.BlockSpec((1, d), lambda i, j: (0, 0)),
            pl.BlockSpec((d, tf), lambda i, j: (0, j)),
            pl.BlockSpec((tf, d), lambda i, j: (j, 0)),
            pl.BlockSpec((1, d), lambda i, j: (0, 0)),
        ],
        out_specs=pl.BlockSpec((tm, d), lambda i, j: (i, 0)),
        out_shape=jax.ShapeDtypeStruct((n, d), F32),
        scratch_shapes=[pltpu.VMEM((tm, d), BF16), pltpu.VMEM((tm, d), F32)],
        compiler_params=_cparams(("parallel", "arbitrary")),
        name="mlp",
    )(x2d, g, wu, wd, gf)


def _col_scale():
    cs = np.ones((1, W_MAIN), np.float32)
    for off in (OFF_QA, OFF_QC):
        cs[0, off:off + BW] = HEAD_DIM ** -0.5
    cs[0, OFF_QB:OFF_QB + N_GROUPS * BW] = HEAD_DIM ** -0.5
    cs[0, OFF_QD:OFF_QD + BW] = DIFF_QK_DIM ** -0.5
    return cs


def kernel(x, mix_norm_g, w_in, b_forget, lambda_q1, lambda_k1, lambda_q2, lambda_k2,
           diff_norm_g, w_branch, w_out, mlp_norm_g, w_up, w_down, final_norm_g):
    bsz, seq, d = x.shape
    depth = w_in.shape[0]
    n = bsz * seq
    tq = min(256, seq)
    tm = min(1024, n)

    slopes = [float(v) for v in _alibi_slopes()]
    n0, n1 = N_HEADS, 2 * N_HEADS
    slopes_dil = (tuple(slopes[:n0]), tuple(slopes[n1:n1 + N_HEADS]), tuple(slopes[n1 + N_HEADS:]))
    slopes_diff = tuple(slopes[n0:n1])

    assert DIL_PATTERNS[-1][0] >= seq and seq % tq == 0 and n % tm == 0
    masks_np, n_off = _dil_plan(tq)
    masks = jnp.asarray(masks_np)
    tril = jnp.asarray(np.tril(np.ones((seq, seq), np.float32)), BF16)
    colscale = jnp.asarray(_col_scale())

    x2d = x.reshape(n, d)
    for l in range(depth):
        lambda_init = 0.8 - 0.6 * math.exp(-0.3 * l)
        wl = w_in[l]
        d_end = F_COL + N_HEADS + 3 * BW
        w_main = jnp.concatenate([wl[:, d_end:], wl[:, :F_COL], wl[:, F_COL + N_HEADS:d_end]],
                                 axis=1).astype(BF16)
        wf = jnp.pad(wl[:, F_COL:F_COL + N_HEADS], ((0, 0), (0, LANES - N_HEADS)))
        wf_hi = wf.astype(BF16)
        wf_lo = (wf - wf_hi.astype(F32)).astype(BF16)
        bf = jnp.pad(b_forget[l].astype(F32), (0, LANES - N_HEADS)).reshape(1, LANES)
        g_mix = mix_norm_g[l].reshape(1, d)

        proj = _norm_proj(x2d, g_mix, w_main, colscale, tm, 512)
        cum = _forget_cum(x2d, g_mix, wf_hi, wf_lo, bf, tril, bsz, tm)
        proj3 = proj.reshape(bsz, seq, W_MAIN)

        y_a = _sb_attn(proj3, tq)
        y_b = _dil_attn(proj3, masks, n_off, tq, slopes_dil)
        y_c = _fox_attn(proj3, cum, tq)
        lam_params = jnp.stack([lambda_q1[l], lambda_k1[l], lambda_q2[l], lambda_k2[l]]).astype(F32)
        g2 = jnp.tile(diff_norm_g[l].astype(F32), 2).reshape(1, LANES)
        y_d = _diff_attn(proj3, lam_params, g2, tq, slopes_diff, 1.0 - lambda_init, lambda_init)

        ys = [y.reshape(n, BW) for y in (y_a, y_b, y_c, y_d)]
        x2d = _merge_out(x2d, ys, proj, w_branch[l].astype(BF16), w_out[l].astype(BF16),
                         min(512, n))
        x2d = _mlp(x2d, mlp_norm_g[l].reshape(1, d), w_up[l].astype(BF16),
                   w_down[l].astype(BF16), final_norm_g.reshape(1, d), tm, 1024,
                   final_norm=(l == depth - 1))
    return x2d.reshape(bsz, seq, d)
```

```python
import functools
import math

import numpy as np
import jax
import jax.numpy as jnp
from jax import lax
from jax.experimental import pallas as pl
from jax.experimental.pallas import tpu as pltpu

F32 = jnp.float32
BF16 = jnp.bfloat16

D_MODEL = 1024
HEAD_DIM = 64
N_HEADS = 4
DIL_PATTERNS = ((128, 1), (512, 4), (2048, 16))
N_GROUPS = len(DIL_PATTERNS)
DIFF_QK_DIM = HEAD_DIM // 2
D_FF = 4 * D_MODEL
RMS_EPS = 1e-6
N_ALIBI = N_GROUPS * N_HEADS + N_HEADS

LANES = 128
BW = N_HEADS * HEAD_DIM
NEG = -1e30
LOG2E = math.log2(math.e)
VMEM_LIMIT = 56 * 1024 * 1024

GATE_W = 4 * D_MODEL
OFF_QA, OFF_KA, OFF_VA = GATE_W, GATE_W + BW, GATE_W + 2 * BW
OFF_QB, OFF_KB, OFF_VB = GATE_W + 3 * BW, GATE_W + 6 * BW, GATE_W + 9 * BW
OFF_QC, OFF_KC, OFF_VC = GATE_W + 12 * BW, GATE_W + 13 * BW, GATE_W + 14 * BW
OFF_QD, OFF_KD, OFF_VD = GATE_W + 15 * BW, GATE_W + 16 * BW, GATE_W + 17 * BW
W_MAIN = GATE_W + 18 * BW
F_COL = 15 * BW


def _alibi_slopes():
    return 2.0 ** (-8.0 * np.arange(1, N_ALIBI + 1) / N_ALIBI)


def _cparams(sem):
    return pltpu.CompilerParams(dimension_semantics=sem, vmem_limit_bytes=VMEM_LIMIT)


def _rms(x, g):
    ms = jnp.mean(x * x, axis=-1, keepdims=True)
    return x * lax.rsqrt(ms + RMS_EPS) * g


def _split3(val):
    p1 = val.astype(BF16).astype(F32)
    r1 = val - p1
    p2 = r1.astype(BF16).astype(F32)
    p3 = (r1 - p2).astype(BF16).astype(F32)
    return p1, p2, p3


def _split2(val):
    hi = val.astype(BF16)
    lo = (val - hi.astype(F32)).astype(BF16)
    return hi, lo


def _dot_nt(a, b):
    return lax.dot_general(a, b, (((1,), (1,)), ((), ())), preferred_element_type=F32)


def _dot(a, b):
    return jnp.dot(a, b, preferred_element_type=F32)


def _fold_max(s):
    m = s[:, :LANES]
    for j in range(1, s.shape[1] // LANES):
        m = jnp.maximum(m, s[:, j * LANES:(j + 1) * LANES])
    return m


def _norm_proj_kernel(x_ref, g_ref, w_ref, cs_ref, o_ref, h_ref):
    @pl.when(pl.program_id(1) == 0)
    def _():
        h_ref[...] = _rms(x_ref[...], g_ref[...]).astype(BF16)

    acc = _dot(h_ref[...], w_ref[...])
    o_ref[...] = (acc * cs_ref[...]).astype(o_ref.dtype)


def _norm_proj(x2d, g, w, colscale, tm, tn):
    n, d = x2d.shape
    wcols = w.shape[1]
    return pl.pallas_call(
        _norm_proj_kernel,
        grid=(n // tm, wcols // tn),
        in_specs=[
            pl.BlockSpec((tm, d), lambda i, j: (i, 0)),
            pl.BlockSpec((1, d), lambda i, j: (0, 0)),
            pl.BlockSpec((d, tn), lambda i, j: (0, j)),
            pl.BlockSpec((1, tn), lambda i, j: (0, j)),
        ],
        out_specs=pl.BlockSpec((tm, tn), lambda i, j: (i, j)),
        out_shape=jax.ShapeDtypeStruct((n, wcols), BF16),
        scratch_shapes=[pltpu.VMEM((tm, d), BF16)],
        compiler_params=_cparams(("parallel", "arbitrary")),
        name="norm_proj",
    )(x2d, g, w, colscale)


def _forget_log_kernel(x_ref, g_ref, whi_ref, wlo_ref, b_ref, o_ref):
    h = _rms(x_ref[...], g_ref[...])
    h_hi, h_lo = _split2(h)
    f = _dot(h_hi, whi_ref[...]) + _dot(h_hi, wlo_ref[...]) + _dot(h_lo, whi_ref[...])
    y = f + b_ref[...]
    o_ref[...] = jnp.minimum(y, 0.0) - jnp.log(1.0 + jnp.exp(-jnp.abs(y)))


def _cumsum_kernel(l_ref, tril_ref, o_ref):
    l1, l2, l3 = _split3(l_ref[0])
    tril = tril_ref[...]
    o_ref[0] = (_dot(tril, l1.astype(BF16)) + _dot(tril, l2.astype(BF16))
                + _dot(tril, l3.astype(BF16)))


def _forget_cum(x2d, g, wf_hi, wf_lo, bf, tril, bsz, tm):
    n, d = x2d.shape
    s = n // bsz
    logf = pl.pallas_call(
        _forget_log_kernel,
        grid=(n // tm,),
        in_specs=[
            pl.BlockSpec((tm, d), lambda i: (i, 0)),
            pl.BlockSpec((1, d), lambda i: (0, 0)),
            pl.BlockSpec((d, LANES), lambda i: (0, 0)),
            pl.BlockSpec((d, LANES), lambda i: (0, 0)),
            pl.BlockSpec((1, LANES), lambda i: (0, 0)),
        ],
        out_specs=pl.BlockSpec((tm, LANES), lambda i: (i, 0)),
        out_shape=jax.ShapeDtypeStruct((n, LANES), F32),
        compiler_params=_cparams(("parallel",)),
        name="forget_log",
    )(x2d, g, wf_hi, wf_lo, bf)
    return pl.pallas_call(
        _cumsum_kernel,
        grid=(bsz,),
        in_specs=[
            pl.BlockSpec((1, s, LANES), lambda i: (i, 0, 0)),
            pl.BlockSpec((s, s), lambda i: (0, 0)),
        ],
        out_specs=pl.BlockSpec((1, s, LANES), lambda i: (i, 0, 0)),
        out_shape=jax.ShapeDtypeStruct((bsz, s, LANES), F32),
        compiler_params=_cparams(("parallel",)),
        name="forget_cumsum",
    )(logf.reshape(bsz, s, LANES), tril)


def _aug(base, keep, lane, lane0, pieces):
    out = jnp.where(keep, base, 0.0)
    for i, p in enumerate(pieces):
        out = jnp.where(lane == lane0 + i, p, out)
    return out


def _ones_at(lane, lane0, n):
    return jnp.where((lane >= lane0) & (lane < lane0 + n), 1.0, 0.0)


def _v_aug(v, lane, h):
    own = (lane < HEAD_DIM) if h == 0 else (lane >= HEAD_DIM)
    return jnp.where(own, v, 1.0).astype(BF16)


def _normalize_pair(acc0, acc1, lane):
    den0 = pltpu.roll(acc0, HEAD_DIM, 1)
    den1 = pltpu.roll(acc1, HEAD_DIM, 1)
    return jnp.where(lane < HEAD_DIM, acc0 / den0, acc1 / den1)


def _two_pass_attention(qa, ka_ref, va_ref, s_ref, v_of_map, i, tq):
    n_maps = len(qa)
    tk = tq

    def score(m, c):
        kc = ka_ref[m, pl.ds(pl.multiple_of(c * tk, tk), tk), :]
        return _dot_nt(qa[m], kc)

    row = lax.broadcasted_iota(jnp.int32, (tq, tk), 0)
    col = lax.broadcasted_iota(jnp.int32, (tq, tk), 1)
    causal = col <= row

    mcol = []
    for m in range(n_maps):
        s = jnp.where(causal, score(m, i), NEG)
        s_ref[m, i] = s
        mcol.append(_fold_max(s))

    def body1(c, carry):
        out = []
        for m in range(n_maps):
            s = score(m, c)
            s_ref[m, c] = s
            out.append(jnp.maximum(carry[m], _fold_max(s)))
        return tuple(out)

    mcol = lax.fori_loop(0, i, body1, tuple(mcol))
    mrow = [jnp.broadcast_to(jnp.max(mc, axis=1, keepdims=True), (tq, LANES)) for mc in mcol]

    def body2(c, accs):
        out = []
        for m in range(n_maps):
            mb = jnp.concatenate([mrow[m]] * (tk // LANES), axis=1)
            pr = jnp.exp2(s_ref[m, c] - mb).astype(BF16)
            vc = va_ref[v_of_map[m], pl.ds(pl.multiple_of(c * tk, tk), tk), :]
            out.append(accs[m] + _dot(pr, vc))
        return tuple(out)

    zeros = tuple(jnp.zeros((tq, LANES), F32) for _ in range(n_maps))
    return lax.fori_loop(0, i + 1, body2, zeros)


def _fox_kernel(q_ref, k_ref, v_ref, cum_ref, o_ref, ka_ref, va_ref, s_ref, *, tq):
    p = pl.program_id(1)
    i = pl.program_id(2)
    s_len = k_ref.shape[1]

    @pl.when(i == 0)
    def _prep():
        lane = lax.broadcasted_iota(jnp.int32, (s_len, LANES), 1)
        k = k_ref[0].astype(F32)
        v = v_ref[0].astype(F32)
        cum = cum_ref[0]
        for h in range(2):
            own = (lane < HEAD_DIM) if h == 0 else (lane >= HEAD_DIM)
            spare = HEAD_DIM if h == 0 else 0
            c = jnp.sum(jnp.where(lane == 2 * p + h, cum, 0.0), axis=1, keepdims=True)
            pieces = _split3(jnp.broadcast_to(c * (-LOG2E), (s_len, LANES)))
            ka_ref[h] = _aug(k, own, lane, spare, pieces).astype(BF16)
            va_ref[h] = _v_aug(v, lane, h)

    lane_q = lax.broadcasted_iota(jnp.int32, (tq, LANES), 1)
    q = q_ref[0].astype(F32)
    qa = []
    for h in range(2):
        own = (lane_q < HEAD_DIM) if h == 0 else (lane_q >= HEAD_DIM)
        spare = HEAD_DIM if h == 0 else 0
        qa.append(jnp.where(own, q, _ones_at(lane_q, spare, 3)).astype(BF16))

    accs = _two_pass_attention(qa, ka_ref, va_ref, s_ref, (0, 1), i, tq)
    o_ref[0] = _normalize_pair(accs[0], accs[1], lane_q).astype(o_ref.dtype)


def _fox_attn(proj, cum, tq):
    b, s, _ = proj.shape
    nq = s // tq
    qc, kc, vc = OFF_QC // LANES, OFF_KC // LANES, OFF_VC // LANES
    return pl.pallas_call(
        functools.partial(_fox_kernel, tq=tq),
        grid=(b, 2, nq),
        in_specs=[
            pl.BlockSpec((1, tq, LANES), lambda bi, p, i: (bi, i, qc + p)),
            pl.BlockSpec((1, s, LANES), lambda bi, p, i: (bi, 0, kc + p)),
            pl.BlockSpec((1, s, LANES), lambda bi, p, i: (bi, 0, vc + p)),
            pl.BlockSpec((1, s, LANES), lambda bi, p, i: (bi, 0, 0)),
        ],
        out_specs=pl.BlockSpec((1, tq, LANES), lambda bi, p, i: (bi, i, p)),
        out_shape=jax.ShapeDtypeStruct((b, s, BW), BF16),
        scratch_shapes=[
            pltpu.VMEM((2, s, LANES), BF16),
            pltpu.VMEM((2, s, LANES), BF16),
            pltpu.VMEM((2, nq, tq, tq), F32),
        ],
        compiler_params=_cparams(("parallel", "parallel", "arbitrary")),
        name="fox_attn",
    )(proj, proj, proj, cum)


def _diff_kernel(q_ref, k_ref, v_ref, lam_ref, g_ref, o_ref, ka_ref, va_ref, s_ref, *,
                 tq, slopes, out_scale, lambda_init):
    p = pl.program_id(1)
    i = pl.program_id(2)
    s_len = k_ref.shape[1]
    dq = DIFF_QK_DIM

    @pl.when(i == 0)
    def _prep():
        lane = lax.broadcasted_iota(jnp.int32, (s_len, LANES), 1)
        pos = lax.broadcasted_iota(jnp.int32, (s_len, LANES), 0).astype(F32)
        k = k_ref[0].astype(F32)
        v = v_ref[0].astype(F32)
        for h in range(2):
            slope = jnp.where(p == 0, slopes[h], slopes[2 + h]).astype(F32)
            pieces = _split3(pos * (slope * LOG2E))
            for c in range(2):
                m = 2 * h + c
                own = (lane >= m * dq) & (lane < (m + 1) * dq)
                spare = ((m + 1) % 4) * dq
                ka_ref[m] = _aug(k, own, lane, spare, pieces).astype(BF16)
            va_ref[h] = _v_aug(v, lane, h)

    lane_q = lax.broadcasted_iota(jnp.int32, (tq, LANES), 1)
    q = q_ref[0].astype(F32)
    qa = []
    for m in range(4):
        own = (lane_q >= m * dq) & (lane_q < (m + 1) * dq)
        spare = ((m + 1) % 4) * dq
        qa.append(jnp.where(own, q, _ones_at(lane_q, spare, 3)).astype(BF16))

    accs = _two_pass_attention(qa, ka_ref, va_ref, s_ref, (0, 0, 1, 1), i, tq)

    lp = lam_ref[...]
    lam = (jnp.exp(jnp.sum(lp[0:1] * lp[1:2], axis=1, keepdims=True))
           - jnp.exp(jnp.sum(lp[2:3] * lp[3:4], axis=1, keepdims=True)) + lambda_init)
    o1 = _normalize_pair(accs[0], accs[2], lane_q)
    o2 = _normalize_pair(accs[1], accs[3], lane_q)
    o = o1 - lam * o2
    r = lax.broadcasted_iota(jnp.int32, (LANES, LANES), 0) // HEAD_DIM
    cc = lax.broadcasted_iota(jnp.int32, (LANES, LANES), 1) // HEAD_DIM
    same = jnp.where(r == cc, 1.0, 0.0).astype(BF16)
    o2_hi, o2_lo = _split2(o * o)
    ms = (_dot(o2_hi, same) + _dot(o2_lo, same)) * (1.0 / HEAD_DIM)
    y = o * lax.rsqrt(ms + RMS_EPS) * g_ref[...]
    o_ref[0] = (y * out_scale).astype(o_ref.dtype)


def _diff_attn(proj, lam_params, g2, tq, slopes, out_scale, lambda_init):
    b, s, _ = proj.shape
    nq = s // tq
    qc, kc, vc = OFF_QD // LANES, OFF_KD // LANES, OFF_VD // LANES
    return pl.pallas_call(
        functools.partial(_diff_kernel, tq=tq, slopes=slopes, out_scale=out_scale,
                          lambda_init=lambda_init),
        grid=(b, 2, nq),
        in_specs=[
            pl.BlockSpec((1, tq, LANES), lambda bi, p, i: (bi, i, qc + p)),
            pl.BlockSpec((1, s, LANES), lambda bi, p, i: (bi, 0, kc + p)),
            pl.BlockSpec((1, s, LANES), lambda bi, p, i: (bi, 0, vc + p)),
            pl.BlockSpec((4, DIFF_QK_DIM), lambda bi, p, i: (0, 0)),
            pl.BlockSpec((1, LANES), lambda bi, p, i: (0, 0)),
        ],
        out_specs=pl.BlockSpec((1, tq, LANES), lambda bi, p, i: (bi, i, p)),
        out_shape=jax.ShapeDtypeStruct((b, s, BW), BF16),
        scratch_shapes=[
            pltpu.VMEM((4, s, LANES), BF16),
            pltpu.VMEM((2, s, LANES), BF16),
            pltpu.VMEM((4, nq, tq, tq), F32),
        ],
        compiler_params=_cparams(("parallel", "parallel", "arbitrary")),
        name="diff_attn",
    )(proj, proj, proj, lam_params, g2)


def _dil_plan(tq):
    t = np.arange(tq)[:, None]
    s = np.arange(tq)[None, :]
    masks, n_off = [], []
    for g, (window, dil) in enumerate(DIL_PATTERNS):
        n = 2 if g == N_GROUPS - 1 else (window + tq - 1) // tq + 1
        for d in range(n):
            dist = d * tq + t - s
            ok = (dist >= 0) & (dist <= window) & (dist % dil == 0)
            masks.append(np.where(ok, 0.0, NEG).astype(np.float32))
        n_off.append(n)
    return np.stack(masks), tuple(n_off)


def _dil_kernel(*refs, tq, slopes, n_off):
    q_refs, k_refs, v_refs = refs[0:3], refs[3:6], refs[6:9]
    mask_ref, o_ref, ka_ref, va_ref, s_ref = refs[9:]
    p = pl.program_id(1)
    i = pl.program_id(2)
    s_len = k_refs[0].shape[1]
    tk = tq
    g_last = N_GROUPS - 1

    def slope_of(g, h):
        return jnp.where(p == 0, slopes[g][h], slopes[g][2 + h]).astype(F32)

    @pl.when(i == 0)
    def _prep():
        lane = lax.broadcasted_iota(jnp.int32, (s_len, LANES), 1)
        pos = lax.broadcasted_iota(jnp.int32, (s_len, LANES), 0).astype(F32)
        for g in range(N_GROUPS):
            k = k_refs[g][0].astype(F32)
            v = v_refs[g][0].astype(F32)
            for h in range(2):
                own = (lane < HEAD_DIM) if h == 0 else (lane >= HEAD_DIM)
                spare = HEAD_DIM if h == 0 else 0
                pieces = _split3(pos * slope_of(g, h))
                base = jnp.where(own, k, _ones_at(lane, spare + 3, 3))
                ka_ref[2 * g + h] = _aug(base, (lane < spare) | (lane >= spare + 3), lane,
                                         spare, pieces).astype(BF16)
                va_ref[2 * g + h] = _v_aug(v, lane, h)

    lane_q = lax.broadcasted_iota(jnp.int32, (tq, LANES), 1)
    tpos = (lax.broadcasted_iota(jnp.int32, (tq, LANES), 0) + i * tq).astype(F32)
    qa = []
    for g in range(N_GROUPS):
        q = q_refs[g][0].astype(F32)
        for h in range(2):
            own = (lane_q < HEAD_DIM) if h == 0 else (lane_q >= HEAD_DIM)
            spare = HEAD_DIM if h == 0 else 0
            pieces = _split3(-tpos * slope_of(g, h))
            base = jnp.where(own, q, _ones_at(lane_q, spare, 3))
            qa.append(_aug(base, (lane_q < spare + 3) | (lane_q >= spare + 6), lane_q,
                           spare + 3, pieces).astype(BF16))

    def score(gh, kb, mask_idx):
        kc = ka_ref[gh, pl.ds(pl.multiple_of(kb * tk, tk), tk), :]
        return _dot_nt(qa[gh], kc) + mask_ref[mask_idx]

    static = []
    mask_idx = 0
    for g in range(N_GROUPS):
        for d in range(n_off[g]):
            if g < g_last or d == 0:
                static.append((g, d, mask_idx))
            mask_idx += 1
    n_static = len(static)
    far_mask = mask_idx - 1

    mcol = [None, None]
    for slot, (g, d, mi) in enumerate(static):
        kb = jnp.maximum(i - d, 0)
        for h in range(2):
            s = score(2 * g + h, kb, mi)
            if d > 0:
                s = jnp.where(i >= d, s, NEG)
            s_ref[h, slot] = s
            fm = _fold_max(s)
            mcol[h] = fm if mcol[h] is None else jnp.maximum(mcol[h], fm)

    def body1(c, carry):
        out = []
        for h in range(2):
            s = score(2 * g_last + h, c, far_mask)
            s_ref[h, n_static + c] = s
            out.append(jnp.maximum(carry[h], _fold_max(s)))
        return tuple(out)

    mcol = lax.fori_loop(0, i, body1, tuple(mcol))

    accs = []
    for h in range(2):
        m = jnp.max(mcol[h], axis=1, keepdims=True)
        mb = jnp.broadcast_to(m, (tq, tk))
        acc = jnp.zeros((tq, LANES), F32)
        for slot, (g, d, mi) in enumerate(static):
            kb = jnp.maximum(i - d, 0)
            pr = jnp.exp(s_ref[h, slot] - mb).astype(BF16)
            vc = va_ref[2 * g + h, pl.ds(pl.multiple_of(kb * tk, tk), tk), :]
            acc = acc + _dot(pr, vc)

        def body2(c, acc, h=h, mb=mb):
            pr = jnp.exp(s_ref[h, n_static + c] - mb).astype(BF16)
            vc = va_ref[2 * g_last + h, pl.ds(pl.multiple_of(c * tk, tk), tk), :]
            return acc + _dot(pr, vc)

        accs.append(lax.fori_loop(0, i, body2, acc))

    o_ref[0] = _normalize_pair(accs[0], accs[1], lane_q).astype(o_ref.dtype)


def _dil_attn(proj, masks, n_off, tq, slopes):
    b, s, _ = proj.shape
    nq = s // tq
    n_static = sum(n_off[:-1]) + 1
    per_group = BW // LANES

    def spec(rows, off, g, is_q):
        blk = off // LANES + g * per_group
        if is_q:
            return pl.BlockSpec((1, rows, LANES), lambda bi, p, i: (bi, i, blk + p))
        return pl.BlockSpec((1, rows, LANES), lambda bi, p, i: (bi, 0, blk + p))

    in_specs = ([spec(tq, OFF_QB, g, True) for g in range(N_GROUPS)]
                + [spec(s, OFF_KB, g, False) for g in range(N_GROUPS)]
                + [spec(s, OFF_VB, g, False) for g in range(N_GROUPS)]
                + [pl.BlockSpec(masks.shape, lambda bi, p, i: (0, 0, 0))])
    return pl.pallas_call(
        functools.partial(_dil_kernel, tq=tq, slopes=slopes, n_off=n_off),
        grid=(b, 2, nq),
        in_specs=in_specs,
        out_specs=pl.BlockSpec((1, tq, LANES), lambda bi, p, i: (bi, i, p)),
        out_shape=jax.ShapeDtypeStruct((b, s, BW), BF16),
        scratch_shapes=[
            pltpu.VMEM((2 * N_GROUPS, s, LANES), BF16),
            pltpu.VMEM((2 * N_GROUPS, s, LANES), BF16),
            pltpu.VMEM((2, n_static + nq, tq, tq), F32),
        ],
        compiler_params=_cparams(("parallel", "parallel", "arbitrary")),
        name="dil_attn",
    )(*([proj] * 9), masks)


def _sb_kernel(q_ref, k_ref, v_ref, o_ref, *, tq, tk):
    i = pl.program_id(2)
    n_diag = tq // tk
    lane_q = lax.broadcasted_iota(jnp.int32, (tq, LANES), 1)
    q = q_ref[0].astype(F32)
    row = lax.broadcasted_iota(jnp.int32, (tq, tk), 0)
    col = lax.broadcasted_iota(jnp.int32, (tq, tk), 1)
    tr = lax.broadcasted_iota(jnp.int32, (tk, tk), 0)
    tc = lax.broadcasted_iota(jnp.int32, (tk, tk), 1)
    tri = jnp.where(tr >= tc, 1.0, 0.0).astype(BF16)
    qh = [jnp.where((lane_q < HEAD_DIM) if h == 0 else (lane_q >= HEAD_DIM), q, 0.0).astype(BF16)
          for h in range(2)]

    def step(c, carry, strict):
        kc = k_ref[0, pl.ds(pl.multiple_of(c * tk, tk), tk), :]
        vc = v_ref[0, pl.ds(pl.multiple_of(c * tk, tk), tk), :]
        out = []
        for h in range(2):
            run, acc = carry[2 * h], carry[2 * h + 1]
            z = _dot_nt(qh[h], kc)
            lk = -(jnp.maximum(z, 0.0) + jnp.log(1.0 + jnp.exp(-jnp.abs(z))))
            if strict is not None:
                lk = jnp.where(strict, lk, 0.0)
            hi, lo = _split2(lk)
            suffix = _dot(hi, tri) + _dot(lo, tri)
            run_b = jnp.concatenate([run] * (tk // LANES), axis=1)
            a = jnp.exp(z + suffix + run_b)
            if strict is not None:
                a = jnp.where(strict, a, 0.0)
            acc = acc + _dot(a.astype(BF16), vc)
            run = run + jnp.broadcast_to(suffix[:, 0:1], (tq, LANES))
            out += [run, acc]
        return tuple(out)

    carry = tuple(jnp.zeros((tq, LANES), F32) for _ in range(4))
    for d in range(n_diag - 1, -1, -1):
        strict = col + d * tk < row
        carry = step(i * n_diag + d, carry, strict)
    carry = lax.fori_loop(0, i * n_diag, lambda n, cr: step(i * n_diag - 1 - n, cr, None), carry)
    o_ref[0] = jnp.where(lane_q < HEAD_DIM, carry[1], carry[3]).astype(o_ref.dtype)


def _sb_attn(proj, tq, tk):
    b, s, _ = proj.shape
    nq = s // tq
    qc, kc, vc = OFF_QA // LANES, OFF_KA // LANES, OFF_VA // LANES
    return pl.pallas_call(
        functools.partial(_sb_kernel, tq=tq, tk=tk),
        grid=(b, 2, nq),
        in_specs=[
            pl.BlockSpec((1, tq, LANES), lambda bi, p, i: (bi, i, qc + p)),
            pl.BlockSpec((1, s, LANES), lambda bi, p, i: (bi, 0, kc + p)),
            pl.BlockSpec((1, s, LANES), lambda bi, p, i: (bi, 0, vc + p)),
        ],
        out_specs=pl.BlockSpec((1, tq, LANES), lambda bi, p, i: (bi, i, p)),
        out_shape=jax.ShapeDtypeStruct((b, s, BW), BF16),
        compiler_params=_cparams(("parallel", "parallel", "arbitrary")),
        name="sb_attn",
    )(proj, proj, proj)


def _merge_kernel(x_ref, ya_ref, yb_ref, yc_ref, yd_ref, gate_ref, wb_ref, wo_ref, o_ref):
    merged = None
    for n, y_ref in enumerate((ya_ref, yb_ref, yc_ref, yd_ref)):
        br = _dot(y_ref[...], wb_ref[n])
        gl = gate_ref[:, n * D_MODEL:(n + 1) * D_MODEL].astype(F32)
        term = br / (1.0 + jnp.exp(-gl))
        merged = term if merged is None else merged + term
    o_ref[...] = x_ref[...] + _dot(merged.astype(BF16), wo_ref[...])


def _merge_out(x2d, ys, proj2d, wb, wo, tm):
    n, d = x2d.shape
    return pl.pallas_call(
        _merge_kernel,
        grid=(n // tm,),
        in_specs=[pl.BlockSpec((tm, d), lambda i: (i, 0))]
        + [pl.BlockSpec((tm, BW), lambda i: (i, 0))] * 4
        + [
            pl.BlockSpec((tm, GATE_W), lambda i: (i, 0)),
            pl.BlockSpec((4, BW, d), lambda i: (0, 0, 0)),
            pl.BlockSpec((d, d), lambda i: (0, 0)),
        ],
        out_specs=pl.BlockSpec((tm, d), lambda i: (i, 0)),
        out_shape=jax.ShapeDtypeStruct((n, d), F32),
        compiler_params=_cparams(("parallel",)),
        name="merge_out",
    )(x2d, *ys, proj2d, wb, wo)


def _mlp_kernel(x_ref, g_ref, wu_ref, wd_ref, gf_ref, o_ref, h_ref, acc_ref, *, final_norm):
    j = pl.program_id(1)

    @pl.when(j == 0)
    def _():
        h_ref[...] = _rms(x_ref[...], g_ref[...]).astype(BF16)

    up = _dot(h_ref[...], wu_ref[...])
    act = jnp.square(jnp.maximum(up, 0.0)).astype(BF16)
    part = _dot(act, wd_ref[...])

    @pl.when(j == 0)
    def _():
        acc_ref[...] = part

    @pl.when(j > 0)
    def _():
        acc_ref[...] += part

    @pl.when(j == pl.num_programs(1) - 1)
    def _():
        y = x_ref[...] + acc_ref[...]
        if final_norm:
            y = _rms(y, gf_ref[...])
        o_ref[...] = y


def _mlp(x2d, g, wu, wd, gf, tm, tf, final_norm):
    n, d = x2d.shape
    dff = wu.shape[1]
    return pl.pallas_call(
        functools.partial(_mlp_kernel, final_norm=final_norm),
        grid=(n // tm, dff // tf),
        in_specs=[
            pl.BlockSpec((tm, d), lambda i, j: (i, 0)),
            pl.BlockSpec((1, d), lambda i, j: (0, 0)),
            pl.BlockSpec((d, tf), lambda i, j: (0, j)),
            pl.BlockSpec((tf, d), lambda i, j: (j, 0)),
            pl.BlockSpec((1, d), lambda i, j: (0, 0)),
        ],
        out_specs=pl.BlockSpec((tm, d), lambda i, j: (i, 0)),
        out_shape=jax.ShapeDtypeStruct((n, d), F32),
        scratch_shapes=[pltpu.VMEM((tm, d), BF16), pltpu.VMEM((tm, d), F32)],
        compiler_params=_cparams(("parallel", "arbitrary")),
        name="mlp",
    )(x2d, g, wu, wd, gf)


def _col_scale():
    cs = np.ones((1, W_MAIN), np.float32)
    cs[0, OFF_QA:OFF_QA + BW] = HEAD_DIM ** -0.5
    cs[0, OFF_QB:OFF_QB + N_GROUPS * BW] = HEAD_DIM ** -0.5
    cs[0, OFF_QC:OFF_QC + BW] = HEAD_DIM ** -0.5 * LOG2E
    cs[0, OFF_QD:OFF_QD + BW] = DIFF_QK_DIM ** -0.5 * LOG2E
    return cs


def kernel(x, mix_norm_g, w_in, b_forget, lambda_q1, lambda_k1, lambda_q2, lambda_k2,
           diff_norm_g, w_branch, w_out, mlp_norm_g, w_up, w_down, final_norm_g):
    bsz, seq, d = x.shape
    depth = w_in.shape[0]
    n = bsz * seq
    tq = min(256, seq)
    tq_big = min(512, seq)
    tm = min(1024, n)

    slopes = [float(v) for v in _alibi_slopes()]
    n0, n1 = N_HEADS, 2 * N_HEADS
    slopes_dil = (tuple(slopes[:n0]), tuple(slopes[n1:n1 + N_HEADS]), tuple(slopes[n1 + N_HEADS:]))
    slopes_diff = tuple(slopes[n0:n1])

    assert DIL_PATTERNS[-1][0] >= seq and seq % tq == 0 and n % tm == 0
    masks_np, n_off = _dil_plan(tq)
    masks = jnp.asarray(masks_np)
    tril = jnp.asarray(np.tril(np.ones((seq, seq), np.float32)), BF16)
    colscale = jnp.asarray(_col_scale())

    x2d = x.reshape(n, d)
    for l in range(depth):
        lambda_init = 0.8 - 0.6 * math.exp(-0.3 * l)
        wl = w_in[l]
        d_end = F_COL + N_HEADS + 3 * BW
        w_main = jnp.concatenate([wl[:, d_end:], wl[:, :F_COL], wl[:, F_COL + N_HEADS:d_end]],
                                 axis=1).astype(BF16)
        wf = jnp.pad(wl[:, F_COL:F_COL + N_HEADS], ((0, 0), (0, LANES - N_HEADS)))
        wf_hi = wf.astype(BF16)
        wf_lo = (wf - wf_hi.astype(F32)).astype(BF16)
        bf = jnp.pad(b_forget[l].astype(F32), (0, LANES - N_HEADS)).reshape(1, LANES)
        g_mix = mix_norm_g[l].reshape(1, d)

        proj = _norm_proj(x2d, g_mix, w_main, colscale, tm, 512)
        cum = _forget_cum(x2d, g_mix, wf_hi, wf_lo, bf, tril, bsz, tm)
        proj3 = proj.reshape(bsz, seq, W_MAIN)

        y_a = _sb_attn(proj3, tq_big, min(256, seq))
        y_b = _dil_attn(proj3, masks, n_off, tq, slopes_dil)
        y_c = _fox_attn(proj3, cum, tq_big)
        lam_params = jnp.stack([lambda_q1[l], lambda_k1[l], lambda_q2[l], lambda_k2[l]]).astype(F32)
        g2 = jnp.tile(diff_norm_g[l].astype(F32), 2).reshape(1, LANES)
        y_d = _diff_attn(proj3, lam_params, g2, tq_big, slopes_diff, 1.0 - lambda_init, lambda_init)

        ys = [y.reshape(n, BW) for y in (y_a, y_b, y_c, y_d)]
        x2d = _merge_out(x2d, ys, proj, w_branch[l].astype(BF16), w_out[l].astype(BF16),
                         min(512, n))
        x2d = _mlp(x2d, mlp_norm_g[l].reshape(1, d), w_up[l].astype(BF16),
                   w_down[l].astype(BF16), final_norm_g.reshape(1, d), tm, 1024,
                   final_norm=(l == depth - 1))
    return x2d.reshape(bsz, seq, d)
```

```python
import functools
import math

import numpy as np
import jax
import jax.numpy as jnp
from jax import lax
from jax.experimental import pallas as pl
from jax.experimental.pallas import tpu as pltpu

F32 = jnp.float32
BF16 = jnp.bfloat16

D_MODEL = 1024
HEAD_DIM = 64
N_HEADS = 4
DIL_PATTERNS = ((128, 1), (512, 4), (2048, 16))
N_GROUPS = len(DIL_PATTERNS)
DIFF_QK_DIM = HEAD_DIM // 2
D_FF = 4 * D_MODEL
RMS_EPS = 1e-6
N_ALIBI = N_GROUPS * N_HEADS + N_HEADS

LANES = 128
BW = N_HEADS * HEAD_DIM
NEG = -1e30
LOG2E = math.log2(math.e)
DB = 128
VMEM_LIMIT = 56 * 1024 * 1024

GATE_W = 4 * D_MODEL
OFF_QA, OFF_KA, OFF_VA = GATE_W, GATE_W + BW, GATE_W + 2 * BW
OFF_QB, OFF_KB, OFF_VB = GATE_W + 3 * BW, GATE_W + 6 * BW, GATE_W + 9 * BW
OFF_QC, OFF_KC, OFF_VC = GATE_W + 12 * BW, GATE_W + 13 * BW, GATE_W + 14 * BW
OFF_QD, OFF_KD, OFF_VD = GATE_W + 15 * BW, GATE_W + 16 * BW, GATE_W + 17 * BW
W_MAIN = GATE_W + 18 * BW
F_COL = 15 * BW


def _alibi_slopes():
    return 2.0 ** (-8.0 * np.arange(1, N_ALIBI + 1) / N_ALIBI)


def _cparams(sem):
    return pltpu.CompilerParams(dimension_semantics=sem, vmem_limit_bytes=VMEM_LIMIT)


def _rms(x, g):
    ms = jnp.mean(x * x, axis=-1, keepdims=True)
    return x * lax.rsqrt(ms + RMS_EPS) * g


def _split3(val):
    p1 = val.astype(BF16).astype(F32)
    r1 = val - p1
    p2 = r1.astype(BF16).astype(F32)
    p3 = (r1 - p2).astype(BF16).astype(F32)
    return p1, p2, p3


def _split2(val):
    hi = val.astype(BF16)
    lo = (val - hi.astype(F32)).astype(BF16)
    return hi, lo


def _dot_nt(a, b):
    return lax.dot_general(a, b, (((1,), (1,)), ((), ())), preferred_element_type=F32)


def _dot(a, b):
    return jnp.dot(a, b, preferred_element_type=F32)


def _fold_max(s):
    m = s[:, :LANES]
    for j in range(1, s.shape[1] // LANES):
        m = jnp.maximum(m, s[:, j * LANES:(j + 1) * LANES])
    return m


def _norm_proj_kernel(x_ref, g_ref, w_ref, cs_ref, o_ref, h_ref):
    @pl.when(pl.program_id(1) == 0)
    def _():
        h_ref[...] = _rms(x_ref[...], g_ref[...]).astype(BF16)

    acc = _dot(h_ref[...], w_ref[...])
    o_ref[...] = (acc * cs_ref[...]).astype(o_ref.dtype)


def _norm_proj(x2d, g, w, colscale, tm, tn):
    n, d = x2d.shape
    wcols = w.shape[1]
    return pl.pallas_call(
        _norm_proj_kernel,
        grid=(n // tm, wcols // tn),
        in_specs=[
            pl.BlockSpec((tm, d), lambda i, j: (i, 0)),
            pl.BlockSpec((1, d), lambda i, j: (0, 0)),
            pl.BlockSpec((d, tn), lambda i, j: (0, j)),
            pl.BlockSpec((1, tn), lambda i, j: (0, j)),
        ],
        out_specs=pl.BlockSpec((tm, tn), lambda i, j: (i, j)),
        out_shape=jax.ShapeDtypeStruct((n, wcols), BF16),
        scratch_shapes=[pltpu.VMEM((tm, d), BF16)],
        compiler_params=_cparams(("parallel", "arbitrary")),
        name="norm_proj",
    )(x2d, g, w, colscale)


def _forget_log_kernel(x_ref, g_ref, whi_ref, wlo_ref, b_ref, o_ref):
    h = _rms(x_ref[...], g_ref[...])
    h_hi, h_lo = _split2(h)
    f = _dot(h_hi, whi_ref[...]) + _dot(h_hi, wlo_ref[...]) + _dot(h_lo, whi_ref[...])
    y = f + b_ref[...]
    o_ref[...] = jnp.minimum(y, 0.0) - jnp.log(1.0 + jnp.exp(-jnp.abs(y)))


def _cumsum_kernel(l_ref, tril_ref, o_ref):
    tril = tril_ref[...]
    blk = tril.shape[0]
    carry = jnp.zeros((1, LANES), F32)
    for j in range(l_ref.shape[1] // blk):
        l1, l2, l3 = _split3(l_ref[0, j * blk:(j + 1) * blk, :])
        c = (_dot(tril, l1.astype(BF16)) + _dot(tril, l2.astype(BF16))
             + _dot(tril, l3.astype(BF16))) + carry
        o_ref[0, j * blk:(j + 1) * blk, :] = c
        carry = c[blk - 1:blk, :]


def _forget_cum(x2d, g, wf_hi, wf_lo, bf, tril, bsz, tm):
    n, d = x2d.shape
    s = n // bsz
    logf = pl.pallas_call(
        _forget_log_kernel,
        grid=(n // tm,),
        in_specs=[
            pl.BlockSpec((tm, d), lambda i: (i, 0)),
            pl.BlockSpec((1, d), lambda i: (0, 0)),
            pl.BlockSpec((d, LANES), lambda i: (0, 0)),
            pl.BlockSpec((d, LANES), lambda i: (0, 0)),
            pl.BlockSpec((1, LANES), lambda i: (0, 0)),
        ],
        out_specs=pl.BlockSpec((tm, LANES), lambda i: (i, 0)),
        out_shape=jax.ShapeDtypeStruct((n, LANES), F32),
        compiler_params=_cparams(("parallel",)),
        name="forget_log",
    )(x2d, g, wf_hi, wf_lo, bf)
    return pl.pallas_call(
        _cumsum_kernel,
        grid=(bsz,),
        in_specs=[
            pl.BlockSpec((1, s, LANES), lambda i: (i, 0, 0)),
            pl.BlockSpec(tril.shape, lambda i: (0, 0)),
        ],
        out_specs=pl.BlockSpec((1, s, LANES), lambda i: (i, 0, 0)),
        out_shape=jax.ShapeDtypeStruct((bsz, s, LANES), F32),
        compiler_params=_cparams(("parallel",)),
        name="forget_cumsum",
    )(logf.reshape(bsz, s, LANES), tril)


def _aug(base, keep, lane, lane0, pieces):
    out = jnp.where(keep, base, 0.0)
    for i, p in enumerate(pieces):
        out = jnp.where(lane == lane0 + i, p, out)
    return out


def _ones_at(lane, lane0, n):
    return jnp.where((lane >= lane0) & (lane < lane0 + n), 1.0, 0.0)


def _v_aug(v, lane, h):
    own = (lane < HEAD_DIM) if h == 0 else (lane >= HEAD_DIM)
    return jnp.where(own, v, 1.0).astype(BF16)


def _normalize_pair(acc0, acc1, lane):
    den0 = pltpu.roll(acc0, HEAD_DIM, 1)
    den1 = pltpu.roll(acc1, HEAD_DIM, 1)
    return jnp.where(lane < HEAD_DIM, acc0 / den0, acc1 / den1)


def _two_pass_attention(qa, ka_ref, va_ref, s_ref, v_of_map, i, tq):
    n_maps = len(qa)
    tk = tq

    def score(m, c):
        kc = ka_ref[m, pl.ds(pl.multiple_of(c * tk, tk), tk), :]
        return _dot_nt(qa[m], kc)

    row = lax.broadcasted_iota(jnp.int32, (tq, tk), 0)
    col = lax.broadcasted_iota(jnp.int32, (tq, tk), 1)
    causal = col <= row

    mcol = []
    for m in range(n_maps):
        s = jnp.where(causal, score(m, i), NEG)
        s_ref[m, i] = s
        mcol.append(_fold_max(s))

    def body1(c, carry):
        out = []
        for m in range(n_maps):
            s = score(m, c)
            s_ref[m, c] = s
            out.append(jnp.maximum(carry[m], _fold_max(s)))
        return tuple(out)

    mcol = lax.fori_loop(0, i, body1, tuple(mcol))
    mrow = [jnp.broadcast_to(jnp.max(mc, axis=1, keepdims=True), (tq, LANES)) for mc in mcol]

    def body2(c, accs):
        out = []
        for m in range(n_maps):
            mb = jnp.concatenate([mrow[m]] * (tk // LANES), axis=1)
            pr = jnp.exp2(s_ref[m, c] - mb).astype(BF16)
            vc = va_ref[v_of_map[m], pl.ds(pl.multiple_of(c * tk, tk), tk), :]
            out.append(accs[m] + _dot(pr, vc))
        return tuple(out)

    zeros = tuple(jnp.zeros((tq, LANES), F32) for _ in range(n_maps))
    return lax.fori_loop(0, i + 1, body2, zeros)


def _fox_kernel(q_ref, k_ref, v_ref, cum_ref, o_ref, ka_ref, va_ref, s_ref, *, tq):
    p = pl.program_id(1)
    i = pl.program_id(2)
    s_len = k_ref.shape[1]

    @pl.when(i == 0)
    def _prep():
        lane = lax.broadcasted_iota(jnp.int32, (s_len, LANES), 1)
        k = k_ref[0].astype(F32)
        v = v_ref[0].astype(F32)
        cum = cum_ref[0]
        for h in range(2):
            own = (lane < HEAD_DIM) if h == 0 else (lane >= HEAD_DIM)
            spare = HEAD_DIM if h == 0 else 0
            c = jnp.sum(jnp.where(lane == 2 * p + h, cum, 0.0), axis=1, keepdims=True)
            pieces = _split3(jnp.broadcast_to(c * (-LOG2E), (s_len, LANES)))
            ka_ref[h] = _aug(k, own, lane, spare, pieces).astype(BF16)
            va_ref[h] = _v_aug(v, lane, h)

    lane_q = lax.broadcasted_iota(jnp.int32, (tq, LANES), 1)
    q = q_ref[0].astype(F32)
    qa = []
    for h in range(2):
        own = (lane_q < HEAD_DIM) if h == 0 else (lane_q >= HEAD_DIM)
        spare = HEAD_DIM if h == 0 else 0
        qa.append(jnp.where(own, q, _ones_at(lane_q, spare, 3)).astype(BF16))

    accs = _two_pass_attention(qa, ka_ref, va_ref, s_ref, (0, 1), i, tq)
    o_ref[0] = _normalize_pair(accs[0], accs[1], lane_q).astype(o_ref.dtype)


def _fox_attn(proj, cum, tq):
    b, s, _ = proj.shape
    nq = s // tq
    qc, kc, vc = OFF_QC // LANES, OFF_KC // LANES, OFF_VC // LANES
    return pl.pallas_call(
        functools.partial(_fox_kernel, tq=tq),
        grid=(b, 2, nq),
        in_specs=[
            pl.BlockSpec((1, tq, LANES), lambda bi, p, i: (bi, i, qc + p)),
            pl.BlockSpec((1, s, LANES), lambda bi, p, i: (bi, 0, kc + p)),
            pl.BlockSpec((1, s, LANES), lambda bi, p, i: (bi, 0, vc + p)),
            pl.BlockSpec((1, s, LANES), lambda bi, p, i: (bi, 0, 0)),
        ],
        out_specs=pl.BlockSpec((1, tq, LANES), lambda bi, p, i: (bi, i, p)),
        out_shape=jax.ShapeDtypeStruct((b, s, BW), BF16),
        scratch_shapes=[
            pltpu.VMEM((2, s, LANES), BF16),
            pltpu.VMEM((2, s, LANES), BF16),
            pltpu.VMEM((2, nq, tq, tq), F32),
        ],
        compiler_params=_cparams(("parallel", "parallel", "arbitrary")),
        name="fox_attn",
    )(proj, proj, proj, cum)


def _diff_kernel(q_ref, k_ref, v_ref, lam_ref, g_ref, o_ref, ka_ref, va_ref, s_ref, *,
                 tq, slopes, out_scale, lambda_init):
    p = pl.program_id(1)
    i = pl.program_id(2)
    s_len = k_ref.shape[1]
    dq = DIFF_QK_DIM

    @pl.when(i == 0)
    def _prep():
        lane = lax.broadcasted_iota(jnp.int32, (s_len, LANES), 1)
        pos = lax.broadcasted_iota(jnp.int32, (s_len, LANES), 0).astype(F32)
        k = k_ref[0].astype(F32)
        v = v_ref[0].astype(F32)
        for h in range(2):
            slope = jnp.where(p == 0, slopes[h], slopes[2 + h]).astype(F32)
            pieces = _split3(pos * (slope * LOG2E))
            for c in range(2):
                m = 2 * h + c
                own = (lane >= m * dq) & (lane < (m + 1) * dq)
                spare = ((m + 1) % 4) * dq
                ka_ref[m] = _aug(k, own, lane, spare, pieces).astype(BF16)
            va_ref[h] = _v_aug(v, lane, h)

    lane_q = lax.broadcasted_iota(jnp.int32, (tq, LANES), 1)
    q = q_ref[0].astype(F32)
    qa = []
    for m in range(4):
        own = (lane_q >= m * dq) & (lane_q < (m + 1) * dq)
        spare = ((m + 1) % 4) * dq
        qa.append(jnp.where(own, q, _ones_at(lane_q, spare, 3)).astype(BF16))

    accs = _two_pass_attention(qa, ka_ref, va_ref, s_ref, (0, 0, 1, 1), i, tq)

    lp = lam_ref[...]
    lam = (jnp.exp(jnp.sum(lp[0:1] * lp[1:2], axis=1, keepdims=True))
           - jnp.exp(jnp.sum(lp[2:3] * lp[3:4], axis=1, keepdims=True)) + lambda_init)
    o1 = _normalize_pair(accs[0], accs[2], lane_q)
    o2 = _normalize_pair(accs[1], accs[3], lane_q)
    o = o1 - lam * o2
    r = lax.broadcasted_iota(jnp.int32, (LANES, LANES), 0) // HEAD_DIM
    cc = lax.broadcasted_iota(jnp.int32, (LANES, LANES), 1) // HEAD_DIM
    same = jnp.where(r == cc, 1.0, 0.0).astype(BF16)
    o2_hi, o2_lo = _split2(o * o)
    ms = (_dot(o2_hi, same) + _dot(o2_lo, same)) * (1.0 / HEAD_DIM)
    y = o * lax.rsqrt(ms + RMS_EPS) * g_ref[...]
    o_ref[0] = (y * out_scale).astype(o_ref.dtype)


def _diff_attn(proj, lam_params, g2, tq, slopes, out_scale, lambda_init):
    b, s, _ = proj.shape
    nq = s // tq
    qc, kc, vc = OFF_QD // LANES, OFF_KD // LANES, OFF_VD // LANES
    return pl.pallas_call(
        functools.partial(_diff_kernel, tq=tq, slopes=slopes, out_scale=out_scale,
                          lambda_init=lambda_init),
        grid=(b, 2, nq),
        in_specs=[
            pl.BlockSpec((1, tq, LANES), lambda bi, p, i: (bi, i, qc + p)),
            pl.BlockSpec((1, s, LANES), lambda bi, p, i: (bi, 0, kc + p)),
            pl.BlockSpec((1, s, LANES), lambda bi, p, i: (bi, 0, vc + p)),
            pl.BlockSpec((4, DIFF_QK_DIM), lambda bi, p, i: (0, 0)),
            pl.BlockSpec((1, LANES), lambda bi, p, i: (0, 0)),
        ],
        out_specs=pl.BlockSpec((1, tq, LANES), lambda bi, p, i: (bi, i, p)),
        out_shape=jax.ShapeDtypeStruct((b, s, BW), BF16),
        scratch_shapes=[
            pltpu.VMEM((4, s, LANES), BF16),
            pltpu.VMEM((2, s, LANES), BF16),
            pltpu.VMEM((4, nq, tq, tq), F32),
        ],
        compiler_params=_cparams(("parallel", "parallel", "arbitrary")),
        name="diff_attn",
    )(proj, proj, proj, lam_params, g2)


def _dil_masks():
    t = np.arange(DB)[:, None]
    s = np.arange(DB)[None, :]
    first = np.where(s <= t, 0.0, NEG)
    s2 = np.arange(2 * DB)[None, :]
    dist = t + DB - s2
    band = np.where((dist >= 0) & (dist <= DB), 0.0, NEG)
    return first.astype(np.float32), band.astype(np.float32)


def _dil_kernel(*refs, slopes):
    q_refs, k_refs, v_refs = refs[0:3], refs[3:6], refs[6:9]
    first_ref, band_ref, o_ref, stage_ref, qa_ref, ka_ref, va_ref, og_ref, lse_ref = refs[9:]
    p = pl.program_id(1)
    s_len = k_refs[0].shape[1]
    n_blk = s_len // DB

    def slope_of(g, h):
        return jnp.where(p == 0, slopes[g][h], slopes[g][2 + h]).astype(F32)

    for g, (window, dil) in enumerate(DIL_PATTERNS):
        cls_len = s_len // dil
        lane = lax.broadcasted_iota(jnp.int32, (cls_len, LANES), 1)
        u = lax.broadcasted_iota(jnp.int32, (cls_len, LANES), 0)
        for kind, src in (("q", q_refs[g]), ("k", k_refs[g]), ("v", v_refs[g])):
            if dil > 1:
                stage_ref[...] = src[0].astype(F32)
            for c in range(dil):
                if dil > 1:
                    x = stage_ref[pl.ds(c, cls_len, stride=dil), :]
                else:
                    x = src[0].astype(F32)
                rows = pl.ds(c * cls_len, cls_len)
                for h in range(2):
                    own = (lane < HEAD_DIM) if h == 0 else (lane >= HEAD_DIM)
                    spare = HEAD_DIM if h == 0 else 0
                    if kind == "q":
                        qa_ref[2 * g + h, rows, :] = jnp.where(
                            own, x, _ones_at(lane, spare, 3)).astype(BF16)
                    elif kind == "k":
                        pos = (u * dil + c).astype(F32)
                        pieces = _split3(pos * (slope_of(g, h) * LOG2E))
                        ka_ref[2 * g + h, rows, :] = _aug(x, own, lane, spare, pieces).astype(BF16)
                    else:
                        va_ref[2 * g + h, rows, :] = _v_aug(x, lane, h)

    lane_b = lax.broadcasted_iota(jnp.int32, (DB, LANES), 1)
    low = lane_b < HEAD_DIM
    for g, (window, dil) in enumerate(DIL_PATTERNS):
        cls_blocks = n_blk // dil
        for j in range(n_blk):
            c, jb = divmod(j, cls_blocks)
            is_first = jb == 0
            k0 = j * DB if is_first else (j - 1) * DB
            kw = DB if is_first else 2 * DB
            mask = first_ref[...] if is_first else band_ref[...]
            accs, ms = [], []
            for h in range(2):
                gh = 2 * g + h
                s = _dot_nt(qa_ref[gh, j * DB:(j + 1) * DB, :], ka_ref[gh, k0:k0 + kw, :]) + mask
                m = jnp.max(s, axis=1, keepdims=True)
                pr = jnp.exp2(s - m).astype(BF16)
                accs.append(_dot(pr, va_ref[gh, k0:k0 + kw, :]))
                ms.append(jnp.broadcast_to(m, (DB, LANES)))
            num = jnp.where(low, accs[0], accs[1])
            den = jnp.where(low, pltpu.roll(accs[0], HEAD_DIM, 1), pltpu.roll(accs[1], HEAD_DIM, 1))
            lse = jnp.where(low, ms[0], ms[1]) + jnp.log(den) * LOG2E
            if dil > 1:
                rows = pl.ds(dil * jb * DB + c, DB, stride=dil)
            else:
                rows = pl.ds(j * DB, DB)
            og_ref[g, rows, :] = num / den
            lse_ref[g, rows, :] = lse

    lane_s = lax.broadcasted_iota(jnp.int32, (s_len, LANES), 1)
    t = lax.broadcasted_iota(jnp.int32, (s_len, LANES), 0).astype(F32)
    lses = []
    for g in range(N_GROUPS):
        slope_lane = jnp.where(lane_s < HEAD_DIM, slope_of(g, 0), slope_of(g, 1)) * LOG2E
        lses.append(lse_ref[g] - slope_lane * t)
    top = jnp.maximum(jnp.maximum(lses[0], lses[1]), lses[2])
    wsum = jnp.zeros((s_len, LANES), F32)
    out = jnp.zeros((s_len, LANES), F32)
    for g in range(N_GROUPS):
        w = jnp.exp2(lses[g] - top)
        wsum = wsum + w
        out = out + w * og_ref[g]
    o_ref[0] = (out / wsum).astype(o_ref.dtype)


def _dil_attn(proj, first, band, slopes):
    b, s, _ = proj.shape
    per_group = BW // LANES

    def spec(off, g):
        blk = off // LANES + g * per_group
        return pl.BlockSpec((1, s, LANES), lambda bi, p: (bi, 0, blk + p))

    in_specs = ([spec(OFF_QB, g) for g in range(N_GROUPS)]
                + [spec(OFF_KB, g) for g in range(N_GROUPS)]
                + [spec(OFF_VB, g) for g in range(N_GROUPS)]
                + [pl.BlockSpec(first.shape, lambda bi, p: (0, 0)),
                   pl.BlockSpec(band.shape, lambda bi, p: (0, 0))])
    return pl.pallas_call(
        functools.partial(_dil_kernel, slopes=slopes),
        grid=(b, 2),
        in_specs=in_specs,
        out_specs=pl.BlockSpec((1, s, LANES), lambda bi, p: (bi, 0, p)),
        out_shape=jax.ShapeDtypeStruct((b, s, BW), BF16),
        scratch_shapes=[
            pltpu.VMEM((s, LANES), F32),
            pltpu.VMEM((2 * N_GROUPS, s, LANES), BF16),
            pltpu.VMEM((2 * N_GROUPS, s, LANES), BF16),
            pltpu.VMEM((2 * N_GROUPS, s, LANES), BF16),
            pltpu.VMEM((N_GROUPS, s, LANES), F32),
            pltpu.VMEM((N_GROUPS, s, LANES), F32),
        ],
        compiler_params=_cparams(("parallel", "parallel")),
        name="dil_attn",
    )(*([proj] * 9), first, band)


def _sb_kernel(q_ref, k_ref, v_ref, o_ref, *, tq, tk):
    i = pl.program_id(2)
    n_diag = tq // tk
    lane_q = lax.broadcasted_iota(jnp.int32, (tq, LANES), 1)
    q = q_ref[0].astype(F32)
    row = lax.broadcasted_iota(jnp.int32, (tq, tk), 0)
    col = lax.broadcasted_iota(jnp.int32, (tq, tk), 1)
    tr = lax.broadcasted_iota(jnp.int32, (tk, tk), 0)
    tc = lax.broadcasted_iota(jnp.int32, (tk, tk), 1)
    tri = jnp.where(tr >= tc, 1.0, 0.0).astype(BF16)
    qn = [jnp.where((lane_q < HEAD_DIM) if h == 0 else (lane_q >= HEAD_DIM), -q, 0.0).astype(BF16)
          for h in range(2)]

    def scores(h, c, strict):
        kc = k_ref[0, pl.ds(pl.multiple_of(c * tk, tk), tk), :]
        zn = _dot_nt(qn[h], kc)
        lk = jnp.minimum(zn, 0.0) - jnp.log(1.0 + jnp.exp(-jnp.abs(zn)))
        if strict is not None:
            lk = jnp.where(strict, lk, 0.0)
        return zn, _dot(lk.astype(BF16), tri)

    def weigh(h, c, strict, zn, suffix, run, acc):
        vc = v_ref[0, pl.ds(pl.multiple_of(c * tk, tk), tk), :]
        run_b = jnp.concatenate([run] * (tk // LANES), axis=1)
        a = jnp.exp(suffix + run_b - zn)
        if strict is not None:
            a = jnp.where(strict, a, 0.0)
        acc = acc + _dot(a.astype(BF16), vc)
        return run + jnp.broadcast_to(suffix[:, 0:1], (tq, LANES)), acc

    def step(chunks, stricts, carry):
        parts = [[scores(h, c, st) for c, st in zip(chunks, stricts)] for h in range(2)]
        out = []
        for h in range(2):
            run, acc = carry[2 * h], carry[2 * h + 1]
            for c, st, (zn, suffix) in zip(chunks, stricts, parts[h]):
                run, acc = weigh(h, c, st, zn, suffix, run, acc)
            out += [run, acc]
        return tuple(out)

    carry = tuple(jnp.zeros((tq, LANES), F32) for _ in range(4))
    base = i * n_diag
    diag = list(range(n_diag - 1, -1, -1))
    carry = step([base + d for d in diag], [col + d * tk < row for d in diag], carry)

    def body(n, cr):
        newest = base - 1 - n * n_diag
        return step([newest - d for d in range(n_diag)], [None] * n_diag, cr)

    carry = lax.fori_loop(0, i, body, carry)
    o_ref[0] = jnp.where(lane_q < HEAD_DIM, carry[1], carry[3]).astype(o_ref.dtype)


def _sb_attn(proj, tq, tk):
    b, s, _ = proj.shape
    nq = s // tq
    qc, kc, vc = OFF_QA // LANES, OFF_KA // LANES, OFF_VA // LANES
    return pl.pallas_call(
        functools.partial(_sb_kernel, tq=tq, tk=tk),
        grid=(b, 2, nq),
        in_specs=[
            pl.BlockSpec((1, tq, LANES), lambda bi, p, i: (bi, i, qc + p)),
            pl.BlockSpec((1, s, LANES), lambda bi, p, i: (bi, 0, kc + p)),
            pl.BlockSpec((1, s, LANES), lambda bi, p, i: (bi, 0, vc + p)),
        ],
        out_specs=pl.BlockSpec((1, tq, LANES), lambda bi, p, i: (bi, i, p)),
        out_shape=jax.ShapeDtypeStruct((b, s, BW), BF16),
        compiler_params=_cparams(("parallel", "parallel", "arbitrary")),
        name="sb_attn",
    )(proj, proj, proj)


def _merge_kernel(x_ref, ya_ref, yb_ref, yc_ref, yd_ref, gate_ref, wb_ref, wo_ref, o_ref):
    merged = None
    for n, y_ref in enumerate((ya_ref, yb_ref, yc_ref, yd_ref)):
        br = _dot(y_ref[...], wb_ref[n])
        gl = gate_ref[:, n * D_MODEL:(n + 1) * D_MODEL].astype(F32)
        term = br * (0.5 + 0.5 * jnp.tanh(0.5 * gl))
        merged = term if merged is None else merged + term
    o_ref[...] = x_ref[...] + _dot(merged.astype(BF16), wo_ref[...])


def _merge_out(x2d, ys, proj2d, wb, wo, tm):
    n, d = x2d.shape
    return pl.pallas_call(
        _merge_kernel,
        grid=(n // tm,),
        in_specs=[pl.BlockSpec((tm, d), lambda i: (i, 0))]
        + [pl.BlockSpec((tm, BW), lambda i: (i, 0))] * 4
        + [
            pl.BlockSpec((tm, GATE_W), lambda i: (i, 0)),
            pl.BlockSpec((4, BW, d), lambda i: (0, 0, 0)),
            pl.BlockSpec((d, d), lambda i: (0, 0)),
        ],
        out_specs=pl.BlockSpec((tm, d), lambda i: (i, 0)),
        out_shape=jax.ShapeDtypeStruct((n, d), F32),
        compiler_params=_cparams(("parallel",)),
        name="merge_out",
    )(x2d, *ys, proj2d, wb, wo)


def _mlp_kernel(x_ref, g_ref, wu_ref, wd_ref, gf_ref, o_ref, h_ref, acc_ref, *, final_norm):
    j = pl.program_id(1)

    @pl.when(j == 0)
    def _():
        h_ref[...] = _rms(x_ref[...], g_ref[...]).astype(BF16)

    up = _dot(h_ref[...], wu_ref[...])
    act = jnp.square(jnp.maximum(up, 0.0)).astype(BF16)
    part = _dot(act, wd_ref[...])

    @pl.when(j == 0)
    def _():
        acc_ref[...] = part

    @pl.when(j > 0)
    def _():
        acc_ref[...] += part

    @pl.when(j == pl.num_programs(1) - 1)
    def _():
        y = x_ref[...] + acc_ref[...]
        if final_norm:
            y = _rms(y, gf_ref[...])
        o_ref[...] = y


def _mlp(x2d, g, wu, wd, gf, tm, tf, final_norm):
    n, d = x2d.shape
    dff = wu.shape[1]
    return pl.pallas_call(
        functools.partial(_mlp_kernel, final_norm=final_norm),
        grid=(n // tm, dff // tf),
        in_specs=[
            pl.BlockSpec((tm, d), lambda i, j: (i, 0)),
            pl.BlockSpec((1, d), lambda i, j: (0, 0)),
            pl.BlockSpec((d, tf), lambda i, j: (0, j)),
            pl.BlockSpec((tf, d), lambda i, j: (j, 0)),
            pl.BlockSpec((1, d), lambda i, j: (0, 0)),
        ],
        out_specs=pl.BlockSpec((tm, d), lambda i, j: (i, 0)),
        out_shape=jax.ShapeDtypeStruct((n, d), F32),
        scratch_shapes=[pltpu.VMEM((tm, d), BF16), pltpu.VMEM((tm, d), F32)],
        compiler_params=_cparams(("parallel", "arbitrary")),
        name="mlp",
    )(x2d, g, wu, wd, gf)


def _col_scale():
    cs = np.ones((1, W_MAIN), np.float32)
    cs[0, OFF_QA:OFF_QA + BW] = HEAD_DIM ** -0.5
    cs[0, OFF_QB:OFF_QB + N_GROUPS * BW] = HEAD_DIM ** -0.5 * LOG2E
    cs[0, OFF_QC:OFF_QC + BW] = HEAD_DIM ** -0.5 * LOG2E
    cs[0, OFF_QD:OFF_QD + BW] = DIFF_QK_DIM ** -0.5 * LOG2E
    return cs


def kernel(x, mix_norm_g, w_in, b_forget, lambda_q1, lambda_k1, lambda_q2, lambda_k2,
           diff_norm_g, w_branch, w_out, mlp_norm_g, w_up, w_down, final_norm_g):
    bsz, seq, d = x.shape
    depth = w_in.shape[0]
    n = bsz * seq
    tq_big = min(512, seq)
    tm = min(1024, n)

    slopes = [float(v) for v in _alibi_slopes()]
    n0, n1 = N_HEADS, 2 * N_HEADS
    slopes_dil = (tuple(slopes[:n0]), tuple(slopes[n1:n1 + N_HEADS]), tuple(slopes[n1 + N_HEADS:]))
    slopes_diff = tuple(slopes[n0:n1])

    assert all(w // r == DB and seq % (DB * r) == 0 for w, r in DIL_PATTERNS) and n % tm == 0
    first, band = (jnp.asarray(m) for m in _dil_masks())
    tril = jnp.asarray(np.tril(np.ones((2 * LANES, 2 * LANES), np.float32)), BF16)
    colscale = jnp.asarray(_col_scale())

    x2d = x.reshape(n, d)
    for l in range(depth):
        lambda_init = 0.8 - 0.6 * math.exp(-0.3 * l)
        wl = w_in[l]
        d_end = F_COL + N_HEADS + 3 * BW
        w_main = jnp.concatenate([wl[:, d_end:], wl[:, :F_COL], wl[:, F_COL + N_HEADS:d_end]],
                                 axis=1).astype(BF16)
        wf = jnp.pad(wl[:, F_COL:F_COL + N_HEADS], ((0, 0), (0, LANES - N_HEADS)))
        wf_hi = wf.astype(BF16)
        wf_lo = (wf - wf_hi.astype(F32)).astype(BF16)
        bf = jnp.pad(b_forget[l].astype(F32), (0, LANES - N_HEADS)).reshape(1, LANES)
        g_mix = mix_norm_g[l].reshape(1, d)

        proj = _norm_proj(x2d, g_mix, w_main, colscale, tm, W_MAIN // 4)
        cum = _forget_cum(x2d, g_mix, wf_hi, wf_lo, bf, tril, bsz, tm)
        proj3 = proj.reshape(bsz, seq, W_MAIN)

        y_a = _sb_attn(proj3, tq_big, min(256, seq))
        y_b = _dil_attn(proj3, first, band, slopes_dil)
        y_c = _fox_attn(proj3, cum, tq_big)
        lam_params = jnp.stack([lambda_q1[l], lambda_k1[l], lambda_q2[l], lambda_k2[l]]).astype(F32)
        g2 = jnp.tile(diff_norm_g[l].astype(F32), 2).reshape(1, LANES)
        y_d = _diff_attn(proj3, lam_params, g2, tq_big, slopes_diff, 1.0 - lambda_init, lambda_init)

        ys = [y.reshape(n, BW) for y in (y_a, y_b, y_c, y_d)]
        x2d = _merge_out(x2d, ys, proj, w_branch[l].astype(BF16), w_out[l].astype(BF16),
                         min(512, n))
        x2d = _mlp(x2d, mlp_norm_g[l].reshape(1, d), w_up[l].astype(BF16),
                   w_down[l].astype(BF16), final_norm_g.reshape(1, d), tm, 1024,
                   final_norm=(l == depth - 1))
    return x2d.reshape(bsz, seq, d)
```

```python
import functools
import math

import numpy as np
import jax
import jax.numpy as jnp
from jax import lax
from jax.experimental import pallas as pl
from jax.experimental.pallas import tpu as pltpu

F32 = jnp.float32
BF16 = jnp.bfloat16

D_MODEL = 1024
HEAD_DIM = 64
N_HEADS = 4
DIL_PATTERNS = ((128, 1), (512, 4), (2048, 16))
N_GROUPS = len(DIL_PATTERNS)
DIFF_QK_DIM = HEAD_DIM // 2
D_FF = 4 * D_MODEL
RMS_EPS = 1e-6
N_ALIBI = N_GROUPS * N_HEADS + N_HEADS

LANES = 128
BW = N_HEADS * HEAD_DIM
NEG = -1e30
LOG2E = math.log2(math.e)
DB = 128
VMEM_LIMIT = 56 * 1024 * 1024

GATE_W = 4 * D_MODEL
OFF_QA, OFF_KA, OFF_VA = GATE_W, GATE_W + BW, GATE_W + 2 * BW
OFF_QB, OFF_KB, OFF_VB = GATE_W + 3 * BW, GATE_W + 6 * BW, GATE_W + 9 * BW
OFF_QC, OFF_KC, OFF_VC = GATE_W + 12 * BW, GATE_W + 13 * BW, GATE_W + 14 * BW
OFF_QD, OFF_KD, OFF_VD = GATE_W + 15 * BW, GATE_W + 16 * BW, GATE_W + 17 * BW
W_MAIN = GATE_W + 18 * BW
F_COL = 15 * BW


def _alibi_slopes():
    return 2.0 ** (-8.0 * np.arange(1, N_ALIBI + 1) / N_ALIBI)


def _cparams(sem):
    return pltpu.CompilerParams(dimension_semantics=sem, vmem_limit_bytes=VMEM_LIMIT)


def _rms(x, g):
    ms = jnp.mean(x * x, axis=-1, keepdims=True)
    return x * lax.rsqrt(ms + RMS_EPS) * g


def _split3(val):
    p1 = val.astype(BF16).astype(F32)
    r1 = val - p1
    p2 = r1.astype(BF16).astype(F32)
    p3 = (r1 - p2).astype(BF16).astype(F32)
    return p1, p2, p3


def _split2(val):
    hi = val.astype(BF16)
    lo = (val - hi.astype(F32)).astype(BF16)
    return hi, lo


def _dot_nt(a, b):
    return lax.dot_general(a, b, (((1,), (1,)), ((), ())), preferred_element_type=F32)


def _dot(a, b):
    return jnp.dot(a, b, preferred_element_type=F32)


def _fold_max(s):
    m = s[:, :LANES]
    for j in range(1, s.shape[1] // LANES):
        m = jnp.maximum(m, s[:, j * LANES:(j + 1) * LANES])
    return m


def _norm_proj_kernel(x_ref, g_ref, w_ref, cs_ref, o_ref, *, tn):
    h = _rms(x_ref[...], g_ref[...]).astype(BF16)
    for c in range(w_ref.shape[1] // tn):
        cols = slice(c * tn, (c + 1) * tn)
        o_ref[:, cols] = (_dot(h, w_ref[:, cols]) * cs_ref[:, cols]).astype(o_ref.dtype)


def _norm_proj(x2d, g, w, colscale, tm, tn):
    n, d = x2d.shape
    wcols = w.shape[1]
    resident = pl.Buffered(1)
    return pl.pallas_call(
        functools.partial(_norm_proj_kernel, tn=tn),
        grid=(n // tm,),
        in_specs=[
            pl.BlockSpec((tm, d), lambda i: (i, 0)),
            pl.BlockSpec((1, d), lambda i: (0, 0)),
            pl.BlockSpec((d, wcols), lambda i: (0, 0), pipeline_mode=resident),
            pl.BlockSpec((1, wcols), lambda i: (0, 0)),
        ],
        out_specs=pl.BlockSpec((tm, wcols), lambda i: (i, 0)),
        out_shape=jax.ShapeDtypeStruct((n, wcols), BF16),
        compiler_params=_cparams(("parallel",)),
        name="norm_proj",
    )(x2d, g, w, colscale)


def _forget_log_kernel(x_ref, g_ref, whi_ref, wlo_ref, b_ref, o_ref):
    h = _rms(x_ref[...], g_ref[...])
    h_hi, h_lo = _split2(h)
    f = _dot(h_hi, whi_ref[...]) + _dot(h_hi, wlo_ref[...]) + _dot(h_lo, whi_ref[...])
    y = f + b_ref[...]
    o_ref[...] = jnp.minimum(y, 0.0) - jnp.log(1.0 + jnp.exp(-jnp.abs(y)))


def _cumsum_kernel(l_ref, tril_ref, o_ref):
    tril = tril_ref[...]
    blk = tril.shape[0]
    carry = jnp.zeros((1, LANES), F32)
    for j in range(l_ref.shape[1] // blk):
        l1, l2, l3 = _split3(l_ref[0, j * blk:(j + 1) * blk, :])
        c = (_dot(tril, l1.astype(BF16)) + _dot(tril, l2.astype(BF16))
             + _dot(tril, l3.astype(BF16))) + carry
        o_ref[0, j * blk:(j + 1) * blk, :] = c
        carry = c[blk - 1:blk, :]


def _forget_cum(x2d, g, wf_hi, wf_lo, bf, tril, bsz, tm):
    n, d = x2d.shape
    s = n // bsz
    logf = pl.pallas_call(
        _forget_log_kernel,
        grid=(n // tm,),
        in_specs=[
            pl.BlockSpec((tm, d), lambda i: (i, 0)),
            pl.BlockSpec((1, d), lambda i: (0, 0)),
            pl.BlockSpec((d, LANES), lambda i: (0, 0)),
            pl.BlockSpec((d, LANES), lambda i: (0, 0)),
            pl.BlockSpec((1, LANES), lambda i: (0, 0)),
        ],
        out_specs=pl.BlockSpec((tm, LANES), lambda i: (i, 0)),
        out_shape=jax.ShapeDtypeStruct((n, LANES), F32),
        compiler_params=_cparams(("parallel",)),
        name="forget_log",
    )(x2d, g, wf_hi, wf_lo, bf)
    return pl.pallas_call(
        _cumsum_kernel,
        grid=(bsz,),
        in_specs=[
            pl.BlockSpec((1, s, LANES), lambda i: (i, 0, 0)),
            pl.BlockSpec(tril.shape, lambda i: (0, 0)),
        ],
        out_specs=pl.BlockSpec((1, s, LANES), lambda i: (i, 0, 0)),
        out_shape=jax.ShapeDtypeStruct((bsz, s, LANES), F32),
        compiler_params=_cparams(("parallel",)),
        name="forget_cumsum",
    )(logf.reshape(bsz, s, LANES), tril)


def _aug(base, keep, lane, lane0, pieces):
    out = jnp.where(keep, base, 0.0)
    for i, p in enumerate(pieces):
        out = jnp.where(lane == lane0 + i, p, out)
    return out


def _ones_at(lane, lane0, n):
    return jnp.where((lane >= lane0) & (lane < lane0 + n), 1.0, 0.0)


def _v_aug(v, lane, h):
    own = (lane < HEAD_DIM) if h == 0 else (lane >= HEAD_DIM)
    return jnp.where(own, v, 1.0).astype(BF16)


def _normalize_pair(acc0, acc1, lane):
    den0 = pltpu.roll(acc0, HEAD_DIM, 1)
    den1 = pltpu.roll(acc1, HEAD_DIM, 1)
    return jnp.where(lane < HEAD_DIM, acc0 / den0, acc1 / den1)


def _two_pass_attention(qa, ka_ref, va_ref, s_ref, v_of_map, i, tq, finish):
    n_maps = len(qa)
    tk = tq
    nq = ka_ref.shape[1] // tk
    row = lax.broadcasted_iota(jnp.int32, (tq, tk), 0)
    col = lax.broadcasted_iota(jnp.int32, (tq, tk), 1)
    causal = col <= row

    def block(k):
        mcol = [None] * n_maps
        for c in range(k + 1):
            for m in range(n_maps):
                s = _dot_nt(qa[m], ka_ref[m, c * tk:(c + 1) * tk, :])
                if c == k:
                    s = jnp.where(causal, s, NEG)
                s_ref[m, c] = s
                fm = _fold_max(s)
                mcol[m] = fm if mcol[m] is None else jnp.maximum(mcol[m], fm)
        accs = []
        for m in range(n_maps):
            mrow = jnp.broadcast_to(jnp.max(mcol[m], axis=1, keepdims=True), (tq, LANES))
            mb = jnp.concatenate([mrow] * (tk // LANES), axis=1)
            acc = None
            for c in range(k + 1):
                pr = jnp.exp2(s_ref[m, c] - mb).astype(BF16)
                part = _dot(pr, va_ref[v_of_map[m], c * tk:(c + 1) * tk, :])
                acc = part if acc is None else acc + part
            accs.append(acc)
        finish(accs)

    for k in range(nq):
        pl.when(i == k)(functools.partial(block, k))


def _fox_kernel(q_ref, k_ref, v_ref, cum_ref, o_ref, ka_ref, va_ref, s_ref, *, tq):
    p = pl.program_id(1)
    i = pl.program_id(2)
    s_len = k_ref.shape[1]

    @pl.when(i == 0)
    def _prep():
        lane = lax.broadcasted_iota(jnp.int32, (s_len, LANES), 1)
        k = k_ref[0].astype(F32)
        v = v_ref[0].astype(F32)
        cum = cum_ref[0]
        for h in range(2):
            own = (lane < HEAD_DIM) if h == 0 else (lane >= HEAD_DIM)
            spare = HEAD_DIM if h == 0 else 0
            c = jnp.sum(jnp.where(lane == 2 * p + h, cum, 0.0), axis=1, keepdims=True)
            pieces = _split3(jnp.broadcast_to(c * (-LOG2E), (s_len, LANES)))
            ka_ref[h] = _aug(k, own, lane, spare, pieces).astype(BF16)
            va_ref[h] = _v_aug(v, lane, h)

    lane_q = lax.broadcasted_iota(jnp.int32, (tq, LANES), 1)
    q = q_ref[0].astype(F32)
    qa = []
    for h in range(2):
        own = (lane_q < HEAD_DIM) if h == 0 else (lane_q >= HEAD_DIM)
        spare = HEAD_DIM if h == 0 else 0
        qa.append(jnp.where(own, q, _ones_at(lane_q, spare, 3)).astype(BF16))

    def finish(accs):
        o_ref[0] = _normalize_pair(accs[0], accs[1], lane_q).astype(o_ref.dtype)

    _two_pass_attention(qa, ka_ref, va_ref, s_ref, (0, 1), i, tq, finish)


def _fox_attn(proj, cum, tq):
    b, s, _ = proj.shape
    nq = s // tq
    qc, kc, vc = OFF_QC // LANES, OFF_KC // LANES, OFF_VC // LANES
    return pl.pallas_call(
        functools.partial(_fox_kernel, tq=tq),
        grid=(b, 2, nq),
        in_specs=[
            pl.BlockSpec((1, tq, LANES), lambda bi, p, i: (bi, i, qc + p)),
            pl.BlockSpec((1, s, LANES), lambda bi, p, i: (bi, 0, kc + p)),
            pl.BlockSpec((1, s, LANES), lambda bi, p, i: (bi, 0, vc + p)),
            pl.BlockSpec((1, s, LANES), lambda bi, p, i: (bi, 0, 0)),
        ],
        out_specs=pl.BlockSpec((1, tq, LANES), lambda bi, p, i: (bi, i, p)),
        out_shape=jax.ShapeDtypeStruct((b, s, BW), BF16),
        scratch_shapes=[
            pltpu.VMEM((2, s, LANES), BF16),
            pltpu.VMEM((2, s, LANES), BF16),
            pltpu.VMEM((2, nq, tq, tq), F32),
        ],
        compiler_params=_cparams(("parallel", "parallel", "arbitrary")),
        name="fox_attn",
    )(proj, proj, proj, cum)


def _diff_kernel(q_ref, k_ref, v_ref, lam_ref, g_ref, o_ref, ka_ref, va_ref, s_ref, *,
                 tq, slopes, out_scale, lambda_init):
    p = pl.program_id(1)
    i = pl.program_id(2)
    s_len = k_ref.shape[1]
    dq = DIFF_QK_DIM

    @pl.when(i == 0)
    def _prep():
        lane = lax.broadcasted_iota(jnp.int32, (s_len, LANES), 1)
        pos = lax.broadcasted_iota(jnp.int32, (s_len, LANES), 0).astype(F32)
        k = k_ref[0].astype(F32)
        v = v_ref[0].astype(F32)
        for h in range(2):
            slope = jnp.where(p == 0, slopes[h], slopes[2 + h]).astype(F32)
            pieces = _split3(pos * (slope * LOG2E))
            for c in range(2):
                m = 2 * h + c
                own = (lane >= m * dq) & (lane < (m + 1) * dq)
                spare = ((m + 1) % 4) * dq
                ka_ref[m] = _aug(k, own, lane, spare, pieces).astype(BF16)
            va_ref[h] = _v_aug(v, lane, h)

    lane_q = lax.broadcasted_iota(jnp.int32, (tq, LANES), 1)
    q = q_ref[0].astype(F32)
    qa = []
    for m in range(4):
        own = (lane_q >= m * dq) & (lane_q < (m + 1) * dq)
        spare = ((m + 1) % 4) * dq
        qa.append(jnp.where(own, q, _ones_at(lane_q, spare, 3)).astype(BF16))

    lp = lam_ref[...]
    lam = (jnp.exp(jnp.sum(lp[0:1] * lp[1:2], axis=1, keepdims=True))
           - jnp.exp(jnp.sum(lp[2:3] * lp[3:4], axis=1, keepdims=True)) + lambda_init)
    r = lax.broadcasted_iota(jnp.int32, (LANES, LANES), 0) // HEAD_DIM
    cc = lax.broadcasted_iota(jnp.int32, (LANES, LANES), 1) // HEAD_DIM
    same = jnp.where(r == cc, 1.0, 0.0).astype(BF16)

    def finish(accs):
        o1 = _normalize_pair(accs[0], accs[2], lane_q)
        o2 = _normalize_pair(accs[1], accs[3], lane_q)
        o = o1 - lam * o2
        o2_hi, o2_lo = _split2(o * o)
        ms = (_dot(o2_hi, same) + _dot(o2_lo, same)) * (1.0 / HEAD_DIM)
        y = o * lax.rsqrt(ms + RMS_EPS) * g_ref[...]
        o_ref[0] = (y * out_scale).astype(o_ref.dtype)

    _two_pass_attention(qa, ka_ref, va_ref, s_ref, (0, 0, 1, 1), i, tq, finish)


def _diff_attn(proj, lam_params, g2, tq, slopes, out_scale, lambda_init):
    b, s, _ = proj.shape
    nq = s // tq
    qc, kc, vc = OFF_QD // LANES, OFF_KD // LANES, OFF_VD // LANES
    return pl.pallas_call(
        functools.partial(_diff_kernel, tq=tq, slopes=slopes, out_scale=out_scale,
                          lambda_init=lambda_init),
        grid=(b, 2, nq),
        in_specs=[
            pl.BlockSpec((1, tq, LANES), lambda bi, p, i: (bi, i, qc + p)),
            pl.BlockSpec((1, s, LANES), lambda bi, p, i: (bi, 0, kc + p)),
            pl.BlockSpec((1, s, LANES), lambda bi, p, i: (bi, 0, vc + p)),
            pl.BlockSpec((4, DIFF_QK_DIM), lambda bi, p, i: (0, 0)),
            pl.BlockSpec((1, LANES), lambda bi, p, i: (0, 0)),
        ],
        out_specs=pl.BlockSpec((1, tq, LANES), lambda bi, p, i: (bi, i, p)),
        out_shape=jax.ShapeDtypeStruct((b, s, BW), BF16),
        scratch_shapes=[
            pltpu.VMEM((4, s, LANES), BF16),
            pltpu.VMEM((2, s, LANES), BF16),
            pltpu.VMEM((4, nq, tq, tq), F32),
        ],
        compiler_params=_cparams(("parallel", "parallel", "arbitrary")),
        name="diff_attn",
    )(proj, proj, proj, lam_params, g2)


def _dil_masks():
    t = np.arange(DB)[:, None]
    s = np.arange(DB)[None, :]
    first = np.where(s <= t, 0.0, NEG)
    s2 = np.arange(2 * DB)[None, :]
    dist = t + DB - s2
    band = np.where((dist >= 0) & (dist <= DB), 0.0, NEG)
    return first.astype(np.float32), band.astype(np.float32)


def _dil_kernel(*refs, slopes):
    q_refs, k_refs, v_refs = refs[0:3], refs[3:6], refs[6:9]
    first_ref, band_ref, o_ref, stage_ref, qa_ref, ka_ref, va_ref, og_ref, lse_ref = refs[9:]
    p = pl.program_id(1)
    s_len = k_refs[0].shape[1]
    n_blk = s_len // DB

    def slope_of(g, h):
        return jnp.where(p == 0, slopes[g][h], slopes[g][2 + h]).astype(F32)

    for g, (window, dil) in enumerate(DIL_PATTERNS):
        cls_len = s_len // dil
        lane = lax.broadcasted_iota(jnp.int32, (cls_len, LANES), 1)
        u = lax.broadcasted_iota(jnp.int32, (cls_len, LANES), 0)
        for kind, src in (("q", q_refs[g]), ("k", k_refs[g]), ("v", v_refs[g])):
            if dil > 1:
                stage_ref[...] = src[0].astype(F32)
            for c in range(dil):
                if dil > 1:
                    x = stage_ref[pl.ds(c, cls_len, stride=dil), :]
                else:
                    x = src[0].astype(F32)
                rows = pl.ds(c * cls_len, cls_len)
                for h in range(2):
                    own = (lane < HEAD_DIM) if h == 0 else (lane >= HEAD_DIM)
                    spare = HEAD_DIM if h == 0 else 0
                    if kind == "q":
                        qa_ref[2 * g + h, rows, :] = jnp.where(
                            own, x, _ones_at(lane, spare, 3)).astype(BF16)
                    elif kind == "k":
                        pos = (u * dil + c).astype(F32)
                        pieces = _split3(pos * (slope_of(g, h) * LOG2E))
                        ka_ref[2 * g + h, rows, :] = _aug(x, own, lane, spare, pieces).astype(BF16)
                    else:
                        va_ref[2 * g + h, rows, :] = _v_aug(x, lane, h)

    lane_b = lax.broadcasted_iota(jnp.int32, (DB, LANES), 1)
    low = lane_b < HEAD_DIM
    for g, (window, dil) in enumerate(DIL_PATTERNS):
        cls_blocks = n_blk // dil
        for j in range(n_blk):
            c, jb = divmod(j, cls_blocks)
            is_first = jb == 0
            k0 = j * DB if is_first else (j - 1) * DB
            kw = DB if is_first else 2 * DB
            mask = first_ref[...] if is_first else band_ref[...]
            accs, ms = [], []
            for h in range(2):
                gh = 2 * g + h
                s = _dot_nt(qa_ref[gh, j * DB:(j + 1) * DB, :], ka_ref[gh, k0:k0 + kw, :]) + mask
                m = jnp.max(s, axis=1, keepdims=True)
                pr = jnp.exp2(s - m).astype(BF16)
                accs.append(_dot(pr, va_ref[gh, k0:k0 + kw, :]))
                ms.append(jnp.broadcast_to(m, (DB, LANES)))
            num = jnp.where(low, accs[0], accs[1])
            den = jnp.where(low, pltpu.roll(accs[0], HEAD_DIM, 1), pltpu.roll(accs[1], HEAD_DIM, 1))
            lse = jnp.where(low, ms[0], ms[1]) + jnp.log(den) * LOG2E
            if dil > 1:
                rows = pl.ds(dil * jb * DB + c, DB, stride=dil)
            else:
                rows = pl.ds(j * DB, DB)
            og_ref[g, rows, :] = num / den
            lse_ref[g, rows, :] = lse

    lane_s = lax.broadcasted_iota(jnp.int32, (s_len, LANES), 1)
    t = lax.broadcasted_iota(jnp.int32, (s_len, LANES), 0).astype(F32)
    lses = []
    for g in range(N_GROUPS):
        slope_lane = jnp.where(lane_s < HEAD_DIM, slope_of(g, 0), slope_of(g, 1)) * LOG2E
        lses.append(lse_ref[g] - slope_lane * t)
    top = jnp.maximum(jnp.maximum(lses[0], lses[1]), lses[2])
    wsum = jnp.zeros((s_len, LANES), F32)
    out = jnp.zeros((s_len, LANES), F32)
    for g in range(N_GROUPS):
        w = jnp.exp2(lses[g] - top)
        wsum = wsum + w
        out = out + w * og_ref[g]
    o_ref[0] = (out / wsum).astype(o_ref.dtype)


def _dil_attn(proj, first, band, slopes):
    b, s, _ = proj.shape
    per_group = BW // LANES

    def spec(off, g):
        blk = off // LANES + g * per_group
        return pl.BlockSpec((1, s, LANES), lambda bi, p: (bi, 0, blk + p))

    in_specs = ([spec(OFF_QB, g) for g in range(N_GROUPS)]
                + [spec(OFF_KB, g) for g in range(N_GROUPS)]
                + [spec(OFF_VB, g) for g in range(N_GROUPS)]
                + [pl.BlockSpec(first.shape, lambda bi, p: (0, 0)),
                   pl.BlockSpec(band.shape, lambda bi, p: (0, 0))])
    return pl.pallas_call(
        functools.partial(_dil_kernel, slopes=slopes),
        grid=(b, 2),
        in_specs=in_specs,
        out_specs=pl.BlockSpec((1, s, LANES), lambda bi, p: (bi, 0, p)),
        out_shape=jax.ShapeDtypeStruct((b, s, BW), BF16),
        scratch_shapes=[
            pltpu.VMEM((s, LANES), F32),
            pltpu.VMEM((2 * N_GROUPS, s, LANES), BF16),
            pltpu.VMEM((2 * N_GROUPS, s, LANES), BF16),
            pltpu.VMEM((2 * N_GROUPS, s, LANES), BF16),
            pltpu.VMEM((N_GROUPS, s, LANES), F32),
            pltpu.VMEM((N_GROUPS, s, LANES), F32),
        ],
        compiler_params=_cparams(("parallel", "parallel")),
        name="dil_attn",
    )(*([proj] * 9), first, band)


def _sb_kernel(q_ref, k_ref, v_ref, o_ref, *, tq, tk):
    i = pl.program_id(2)
    n_diag = tq // tk
    lane_q = lax.broadcasted_iota(jnp.int32, (tq, LANES), 1)
    q = q_ref[0].astype(F32)
    row = lax.broadcasted_iota(jnp.int32, (tq, tk), 0)
    col = lax.broadcasted_iota(jnp.int32, (tq, tk), 1)
    tr = lax.broadcasted_iota(jnp.int32, (tk, tk), 0)
    tc = lax.broadcasted_iota(jnp.int32, (tk, tk), 1)
    tri = jnp.where(tr >= tc, 1.0, 0.0).astype(BF16)
    qn = [jnp.where((lane_q < HEAD_DIM) if h == 0 else (lane_q >= HEAD_DIM), -q, 0.0).astype(BF16)
          for h in range(2)]

    def scores(h, c, strict):
        kc = k_ref[0, c * tk:(c + 1) * tk, :]
        zn = _dot_nt(qn[h], kc)
        lk = jnp.minimum(zn, 0.0) - jnp.log(1.0 + jnp.exp(-jnp.abs(zn)))
        if strict is not None:
            lk = jnp.where(strict, lk, 0.0)
        return zn, _dot(lk.astype(BF16), tri)

    def weigh(h, c, strict, zn, suffix, run, acc):
        vc = v_ref[0, c * tk:(c + 1) * tk, :]
        run_b = jnp.concatenate([run] * (tk // LANES), axis=1)
        a = jnp.exp(suffix + run_b - zn)
        if strict is not None:
            a = jnp.where(strict, a, 0.0)
        acc = acc + _dot(a.astype(BF16), vc)
        return run + jnp.broadcast_to(suffix[:, 0:1], (tq, LANES)), acc

    def step(chunks, stricts, carry):
        parts = [[scores(h, c, st) for c, st in zip(chunks, stricts)] for h in range(2)]
        out = []
        for h in range(2):
            run, acc = carry[2 * h], carry[2 * h + 1]
            for c, st, (zn, suffix) in zip(chunks, stricts, parts[h]):
                run, acc = weigh(h, c, st, zn, suffix, run, acc)
            out += [run, acc]
        return tuple(out)

    def block(k):
        carry = tuple(jnp.zeros((tq, LANES), F32) for _ in range(4))
        diag = list(range(n_diag - 1, -1, -1))
        carry = step([k * n_diag + d for d in diag], [col + d * tk < row for d in diag], carry)
        for n in range(k):
            newest = k * n_diag - 1 - n * n_diag
            carry = step([newest - d for d in range(n_diag)], [None] * n_diag, carry)
        o_ref[0] = jnp.where(lane_q < HEAD_DIM, carry[1], carry[3]).astype(o_ref.dtype)

    for k in range(k_ref.shape[1] // tq):
        pl.when(i == k)(functools.partial(block, k))


def _sb_attn(proj, tq, tk):
    b, s, _ = proj.shape
    nq = s // tq
    qc, kc, vc = OFF_QA // LANES, OFF_KA // LANES, OFF_VA // LANES
    return pl.pallas_call(
        functools.partial(_sb_kernel, tq=tq, tk=tk),
        grid=(b, 2, nq),
        in_specs=[
            pl.BlockSpec((1, tq, LANES), lambda bi, p, i: (bi, i, qc + p)),
            pl.BlockSpec((1, s, LANES), lambda bi, p, i: (bi, 0, kc + p)),
            pl.BlockSpec((1, s, LANES), lambda bi, p, i: (bi, 0, vc + p)),
        ],
        out_specs=pl.BlockSpec((1, tq, LANES), lambda bi, p, i: (bi, i, p)),
        out_shape=jax.ShapeDtypeStruct((b, s, BW), BF16),
        compiler_params=_cparams(("parallel", "parallel", "arbitrary")),
        name="sb_attn",
    )(proj, proj, proj)


def _merge_kernel(x_ref, ya_ref, yb_ref, yc_ref, yd_ref, gate_ref, wb_ref, wo_ref, o_ref):
    merged = None
    for n, y_ref in enumerate((ya_ref, yb_ref, yc_ref, yd_ref)):
        br = _dot(y_ref[...], wb_ref[n])
        gl = gate_ref[:, n * D_MODEL:(n + 1) * D_MODEL].astype(F32)
        term = br * (0.5 + 0.5 * jnp.tanh(0.5 * gl))
        merged = term if merged is None else merged + term
    o_ref[...] = x_ref[...] + _dot(merged.astype(BF16), wo_ref[...])


def _merge_out(x2d, ys, proj2d, wb, wo, tm):
    n, d = x2d.shape
    return pl.pallas_call(
        _merge_kernel,
        grid=(n // tm,),
        in_specs=[pl.BlockSpec((tm, d), lambda i: (i, 0))]
        + [pl.BlockSpec((tm, BW), lambda i: (i, 0))] * 4
        + [
            pl.BlockSpec((tm, GATE_W), lambda i: (i, 0)),
            pl.BlockSpec((4, BW, d), lambda i: (0, 0, 0)),
            pl.BlockSpec((d, d), lambda i: (0, 0)),
        ],
        out_specs=pl.BlockSpec((tm, d), lambda i: (i, 0)),
        out_shape=jax.ShapeDtypeStruct((n, d), F32),
        compiler_params=_cparams(("parallel",)),
        name="merge_out",
    )(x2d, *ys, proj2d, wb, wo)


def _mlp_kernel(x_ref, g_ref, wu_ref, wd_ref, gf_ref, o_ref, *, final_norm, tf):
    x = x_ref[...]
    h = _rms(x, g_ref[...]).astype(BF16)
    y = x
    for c in range(wu_ref.shape[1] // tf):
        up = _dot(h, wu_ref[:, c * tf:(c + 1) * tf])
        act = jnp.square(jnp.maximum(up, 0.0)).astype(BF16)
        y = y + _dot(act, wd_ref[c * tf:(c + 1) * tf, :])
    if final_norm:
        y = _rms(y, gf_ref[...])
    o_ref[...] = y


def _mlp(x2d, g, wu, wd, gf, tm, tf, final_norm):
    n, d = x2d.shape
    dff = wu.shape[1]
    resident = pl.Buffered(1)
    return pl.pallas_call(
        functools.partial(_mlp_kernel, final_norm=final_norm, tf=tf),
        grid=(n // tm,),
        in_specs=[
            pl.BlockSpec((tm, d), lambda i: (i, 0)),
            pl.BlockSpec((1, d), lambda i: (0, 0)),
            pl.BlockSpec((d, dff), lambda i: (0, 0), pipeline_mode=resident),
            pl.BlockSpec((dff, d), lambda i: (0, 0), pipeline_mode=resident),
            pl.BlockSpec((1, d), lambda i: (0, 0)),
        ],
        out_specs=pl.BlockSpec((tm, d), lambda i: (i, 0)),
        out_shape=jax.ShapeDtypeStruct((n, d), F32),
        compiler_params=_cparams(("parallel",)),
        name="mlp",
    )(x2d, g, wu, wd, gf)


def _col_scale():
    cs = np.ones((1, W_MAIN), np.float32)
    cs[0, OFF_QA:OFF_QA + BW] = HEAD_DIM ** -0.5
    cs[0, OFF_QB:OFF_QB + N_GROUPS * BW] = HEAD_DIM ** -0.5 * LOG2E
    cs[0, OFF_QC:OFF_QC + BW] = HEAD_DIM ** -0.5 * LOG2E
    cs[0, OFF_QD:OFF_QD + BW] = DIFF_QK_DIM ** -0.5 * LOG2E
    return cs


def kernel(x, mix_norm_g, w_in, b_forget, lambda_q1, lambda_k1, lambda_q2, lambda_k2,
           diff_norm_g, w_branch, w_out, mlp_norm_g, w_up, w_down, final_norm_g):
    bsz, seq, d = x.shape
    depth = w_in.shape[0]
    n = bsz * seq
    tq_big = min(512, seq)
    tm = min(1024, n)

    slopes = [float(v) for v in _alibi_slopes()]
    n0, n1 = N_HEADS, 2 * N_HEADS
    slopes_dil = (tuple(slopes[:n0]), tuple(slopes[n1:n1 + N_HEADS]), tuple(slopes[n1 + N_HEADS:]))
    slopes_diff = tuple(slopes[n0:n1])

    assert all(w // r == DB and seq % (DB * r) == 0 for w, r in DIL_PATTERNS) and n % tm == 0
    first, band = (jnp.asarray(m) for m in _dil_masks())
    tril = jnp.asarray(np.tril(np.ones((2 * LANES, 2 * LANES), np.float32)), BF16)
    colscale = jnp.asarray(_col_scale())

    x2d = x.reshape(n, d)
    for l in range(depth):
        lambda_init = 0.8 - 0.6 * math.exp(-0.3 * l)
        wl = w_in[l]
        d_end = F_COL + N_HEADS + 3 * BW
        w_main = jnp.concatenate([wl[:, d_end:], wl[:, :F_COL], wl[:, F_COL + N_HEADS:d_end]],
                                 axis=1).astype(BF16)
        wf = jnp.pad(wl[:, F_COL:F_COL + N_HEADS], ((0, 0), (0, LANES - N_HEADS)))
        wf_hi = wf.astype(BF16)
        wf_lo = (wf - wf_hi.astype(F32)).astype(BF16)
        bf = jnp.pad(b_forget[l].astype(F32), (0, LANES - N_HEADS)).reshape(1, LANES)
        g_mix = mix_norm_g[l].reshape(1, d)

        proj = _norm_proj(x2d, g_mix, w_main, colscale, min(512, n), 512)
        cum = _forget_cum(x2d, g_mix, wf_hi, wf_lo, bf, tril, bsz, tm)
        proj3 = proj.reshape(bsz, seq, W_MAIN)

        y_a = _sb_attn(proj3, tq_big, min(256, seq))
        y_b = _dil_attn(proj3, first, band, slopes_dil)
        y_c = _fox_attn(proj3, cum, tq_big)
        lam_params = jnp.stack([lambda_q1[l], lambda_k1[l], lambda_q2[l], lambda_k2[l]]).astype(F32)
        g2 = jnp.tile(diff_norm_g[l].astype(F32), 2).reshape(1, LANES)
        y_d = _diff_attn(proj3, lam_params, g2, tq_big, slopes_diff, 1.0 - lambda_init, lambda_init)

        ys = [y.reshape(n, BW) for y in (y_a, y_b, y_c, y_d)]
        x2d = _merge_out(x2d, ys, proj, w_branch[l].astype(BF16), w_out[l].astype(BF16),
                         min(512, n))
        x2d = _mlp(x2d, mlp_norm_g[l].reshape(1, d), w_up[l].astype(BF16),
                   w_down[l].astype(BF16), final_norm_g.reshape(1, d), tm, 1024,
                   final_norm=(l == depth - 1))
    return x2d.reshape(bsz, seq, d)
```

```python
import functools
import math

import ml_dtypes
import numpy as np
import jax
import jax.numpy as jnp
from jax import lax
from jax.experimental import pallas as pl
from jax.experimental.pallas import tpu as pltpu

F32 = jnp.float32
BF16 = jnp.bfloat16

D_MODEL = 1024
HEAD_DIM = 64
N_HEADS = 4
DIL_PATTERNS = ((128, 1), (512, 4), (2048, 16))
N_GROUPS = len(DIL_PATTERNS)
DIFF_QK_DIM = HEAD_DIM // 2
D_FF = 4 * D_MODEL
RMS_EPS = 1e-6
N_ALIBI = N_GROUPS * N_HEADS + N_HEADS

LANES = 128
BW = N_HEADS * HEAD_DIM
NEG = -1e30
LOG2E = math.log2(math.e)
DB = 128
VMEM_LIMIT = 56 * 1024 * 1024

OFF_QA, OFF_KA, OFF_VA = 0, BW, 2 * BW
OFF_QB, OFF_KB, OFF_VB = 3 * BW, 6 * BW, 9 * BW
OFF_QC, OFF_KC, OFF_VC = 12 * BW, 13 * BW, 14 * BW
OFF_QD, OFF_KD, OFF_VD = 15 * BW, 16 * BW, 17 * BW
W_MAIN = 18 * BW
GATE_W = 4 * D_MODEL
F_COL = 15 * BW
N_PIECES = 4


def _alibi_slopes():
    return 2.0 ** (-8.0 * np.arange(1, N_ALIBI + 1) / N_ALIBI)


def _cparams(sem):
    return pltpu.CompilerParams(dimension_semantics=sem, vmem_limit_bytes=VMEM_LIMIT)


def _rms(x, g):
    ms = jnp.mean(x * x, axis=-1, keepdims=True)
    return x * lax.rsqrt(ms + RMS_EPS) * g


def _split3(val):
    p1 = val.astype(BF16).astype(F32)
    r1 = val - p1
    p2 = r1.astype(BF16).astype(F32)
    p3 = (r1 - p2).astype(BF16).astype(F32)
    return p1, p2, p3


def _split2(val):
    hi = val.astype(BF16)
    lo = (val - hi.astype(F32)).astype(BF16)
    return hi, lo


def _dot_nt(a, b):
    return lax.dot_general(a, b, (((1,), (1,)), ((), ())), preferred_element_type=F32)


def _dot(a, b):
    return jnp.dot(a, b, preferred_element_type=F32)


def _fold_max(s):
    m = s[:, :LANES]
    for j in range(1, s.shape[1] // LANES):
        m = jnp.maximum(m, s[:, j * LANES:(j + 1) * LANES])
    return m


def _norm_proj_kernel(x_ref, g_ref, wa_ref, wd_ref, cs_ref, o_ref, g1_ref, g2_ref, stage_ref, *,
                      tn):
    tm = x_ref.shape[0]
    h = _rms(x_ref[...], g_ref[...]).astype(BF16)
    na = wa_ref.shape[1] // tn
    for c in range(na + wd_ref.shape[1] // tn):
        w = wa_ref[:, c * tn:(c + 1) * tn] if c < na else wd_ref[:, (c - na) * tn:(c - na + 1) * tn]
        cols = slice(c * tn, (c + 1) * tn)
        res = _dot(h, w) * cs_ref[:, cols]
        o_ref[:, cols] = res.astype(o_ref.dtype)
        kind = c - OFF_QB // tn
        if 0 <= kind < 3:
            per = BW // LANES
            for j in range((N_GROUPS - 1) * per):
                stage_ref[j] = res[:, BW + j * LANES:BW + (j + 1) * LANES]
            for gi, out_ref in enumerate((g1_ref, g2_ref)):
                dil = DIL_PATTERNS[gi + 1][1]
                for cl in range(dil):
                    for j in range(per):
                        rows = stage_ref[gi * per + j, pl.ds(cl, tm // dil, stride=dil), :]
                        lanes = slice(kind * BW + j * LANES, kind * BW + (j + 1) * LANES)
                        out_ref[0, cl, :, lanes] = rows.astype(out_ref.dtype)


def _norm_proj(x2d, g, w_a, w_d, colscale, bsz, tm):
    n, d = x2d.shape
    s = n // bsz
    per_batch = s // tm
    tn = N_GROUPS * BW
    resident = pl.Buffered(1)
    d1, d2 = DIL_PATTERNS[1][1], DIL_PATTERNS[2][1]
    return pl.pallas_call(
        functools.partial(_norm_proj_kernel, tn=tn),
        grid=(n // tm,),
        in_specs=[
            pl.BlockSpec((tm, d), lambda i: (i, 0)),
            pl.BlockSpec((1, d), lambda i: (0, 0)),
            pl.BlockSpec(w_a.shape, lambda i: (0, 0), pipeline_mode=resident),
            pl.BlockSpec(w_d.shape, lambda i: (0, 0), pipeline_mode=resident),
            pl.BlockSpec((1, W_MAIN), lambda i: (0, 0)),
        ],
        out_specs=[
            pl.BlockSpec((tm, W_MAIN), lambda i: (i, 0)),
            pl.BlockSpec((1, d1, tm // d1, tn), lambda i: (i // per_batch, 0, i % per_batch, 0)),
            pl.BlockSpec((1, d2, tm // d2, tn), lambda i: (i // per_batch, 0, i % per_batch, 0)),
        ],
        out_shape=[
            jax.ShapeDtypeStruct((n, W_MAIN), BF16),
            jax.ShapeDtypeStruct((bsz, d1, s // d1, tn), BF16),
            jax.ShapeDtypeStruct((bsz, d2, s // d2, tn), BF16),
        ],
        scratch_shapes=[pltpu.VMEM(((N_GROUPS - 1) * BW // LANES, tm, LANES), F32)],
        compiler_params=_cparams(("parallel",)),
        name="norm_proj",
    )(x2d, g, w_a, w_d, colscale)


def _forget_log_kernel(x_ref, g_ref, whi_ref, wlo_ref, b_ref, o_ref):
    h = _rms(x_ref[...], g_ref[...])
    h_hi, h_lo = _split2(h)
    f = _dot(h_hi, whi_ref[...]) + _dot(h_hi, wlo_ref[...]) + _dot(h_lo, whi_ref[...])
    y = f + b_ref[...]
    o_ref[...] = jnp.minimum(y, 0.0) - jnp.log(1.0 + jnp.exp(-jnp.abs(y)))


def _cumsum_kernel(l_ref, tril_ref, o_ref):
    tril = tril_ref[...]
    blk = tril.shape[0]
    carry = jnp.zeros((1, LANES), F32)
    for j in range(l_ref.shape[1] // blk):
        l1, l2, l3 = _split3(l_ref[0, j * blk:(j + 1) * blk, :])
        c = (_dot(tril, l1.astype(BF16)) + _dot(tril, l2.astype(BF16))
             + _dot(tril, l3.astype(BF16))) + carry
        o_ref[0, j * blk:(j + 1) * blk, :] = c
        carry = c[blk - 1:blk, :]


def _forget_cum(x2d, g, wf_hi, wf_lo, bf, tril, bsz, tm):
    n, d = x2d.shape
    s = n // bsz
    logf = pl.pallas_call(
        _forget_log_kernel,
        grid=(n // tm,),
        in_specs=[
            pl.BlockSpec((tm, d), lambda i: (i, 0)),
            pl.BlockSpec((1, d), lambda i: (0, 0)),
            pl.BlockSpec((d, LANES), lambda i: (0, 0)),
            pl.BlockSpec((d, LANES), lambda i: (0, 0)),
            pl.BlockSpec((1, LANES), lambda i: (0, 0)),
        ],
        out_specs=pl.BlockSpec((tm, LANES), lambda i: (i, 0)),
        out_shape=jax.ShapeDtypeStruct((n, LANES), F32),
        compiler_params=_cparams(("parallel",)),
        name="forget_log",
    )(x2d, g, wf_hi, wf_lo, bf)
    return pl.pallas_call(
        _cumsum_kernel,
        grid=(bsz,),
        in_specs=[
            pl.BlockSpec((1, s, LANES), lambda i: (i, 0, 0)),
            pl.BlockSpec(tril.shape, lambda i: (0, 0)),
        ],
        out_specs=pl.BlockSpec((1, s, LANES), lambda i: (i, 0, 0)),
        out_shape=jax.ShapeDtypeStruct((bsz, s, LANES), F32),
        compiler_params=_cparams(("parallel",)),
        name="forget_cumsum",
    )(logf.reshape(bsz, s, LANES), tril)


def _aug(base, keep, lane, lane0, pieces):
    out = jnp.where(keep, base, 0.0)
    for i, p in enumerate(pieces):
        out = jnp.where(lane == lane0 + i, p, out)
    return out


def _ones_at(lane, lane0, n):
    return jnp.where((lane >= lane0) & (lane < lane0 + n), 1.0, 0.0)


def _v_aug(v, lane, h):
    own = (lane < HEAD_DIM) if h == 0 else (lane >= HEAD_DIM)
    return jnp.where(own, v, 1.0).astype(BF16)


def _normalize_pair(acc0, acc1, lane):
    den0 = pltpu.roll(acc0, HEAD_DIM, 1)
    den1 = pltpu.roll(acc1, HEAD_DIM, 1)
    return jnp.where(lane < HEAD_DIM, acc0 / den0, acc1 / den1)


def _two_pass_attention(qa, ka_ref, va_ref, s_ref, v_of_map, i, tq, finish):
    n_maps = len(qa)
    tk = tq
    nq = ka_ref.shape[1] // tk
    row = lax.broadcasted_iota(jnp.int32, (tq, tk), 0)
    col = lax.broadcasted_iota(jnp.int32, (tq, tk), 1)
    causal = col <= row

    def block(k):
        mcol = [None] * n_maps
        for c in range(k + 1):
            for m in range(n_maps):
                s = _dot_nt(qa[m], ka_ref[m, c * tk:(c + 1) * tk, :])
                if c == k:
                    s = jnp.where(causal, s, NEG)
                s_ref[m, c] = s
                fm = _fold_max(s)
                mcol[m] = fm if mcol[m] is None else jnp.maximum(mcol[m], fm)
        accs = []
        for m in range(n_maps):
            mrow = jnp.broadcast_to(jnp.max(mcol[m], axis=1, keepdims=True), (tq, LANES))
            mb = jnp.concatenate([mrow] * (tk // LANES), axis=1)
            acc = None
            for c in range(k + 1):
                pr = jnp.exp2(s_ref[m, c] - mb).astype(BF16)
                part = _dot(pr, va_ref[v_of_map[m], c * tk:(c + 1) * tk, :])
                acc = part if acc is None else acc + part
            accs.append(acc)
        finish(accs)

    for k in range(nq):
        pl.when(i == k)(functools.partial(block, k))


def _fox_kernel(q_ref, k_ref, v_ref, cum_ref, o_ref, ka_ref, va_ref, s_ref, *, tq):
    p = pl.program_id(1)
    i = pl.program_id(2)
    s_len = k_ref.shape[1]

    @pl.when(i == 0)
    def _prep():
        lane = lax.broadcasted_iota(jnp.int32, (s_len, LANES), 1)
        k = k_ref[0].astype(F32)
        v = v_ref[0].astype(F32)
        cum = cum_ref[0]
        for h in range(2):
            own = (lane < HEAD_DIM) if h == 0 else (lane >= HEAD_DIM)
            spare = HEAD_DIM if h == 0 else 0
            c = jnp.sum(jnp.where(lane == 2 * p + h, cum, 0.0), axis=1, keepdims=True)
            pieces = _split3(jnp.broadcast_to(c * (-LOG2E), (s_len, LANES)))
            ka_ref[h] = _aug(k, own, lane, spare, pieces).astype(BF16)
            va_ref[h] = _v_aug(v, lane, h)

    lane_q = lax.broadcasted_iota(jnp.int32, (tq, LANES), 1)
    q = q_ref[0].astype(F32)
    qa = []
    for h in range(2):
        own = (lane_q < HEAD_DIM) if h == 0 else (lane_q >= HEAD_DIM)
        spare = HEAD_DIM if h == 0 else 0
        qa.append(jnp.where(own, q, _ones_at(lane_q, spare, 3)).astype(BF16))

    def finish(accs):
        o_ref[0] = _normalize_pair(accs[0], accs[1], lane_q).astype(o_ref.dtype)

    _two_pass_attention(qa, ka_ref, va_ref, s_ref, (0, 1), i, tq, finish)


def _fox_attn(proj, cum, tq):
    b, s, _ = proj.shape
    nq = s // tq
    qc, kc, vc = OFF_QC // LANES, OFF_KC // LANES, OFF_VC // LANES
    return pl.pallas_call(
        functools.partial(_fox_kernel, tq=tq),
        grid=(b, 2, nq),
        in_specs=[
            pl.BlockSpec((1, tq, LANES), lambda bi, p, i: (bi, i, qc + p)),
            pl.BlockSpec((1, s, LANES), lambda bi, p, i: (bi, 0, kc + p)),
            pl.BlockSpec((1, s, LANES), lambda bi, p, i: (bi, 0, vc + p)),
            pl.BlockSpec((1, s, LANES), lambda bi, p, i: (bi, 0, 0)),
        ],
        out_specs=pl.BlockSpec((1, tq, LANES), lambda bi, p, i: (bi, i, p)),
        out_shape=jax.ShapeDtypeStruct((b, s, BW), BF16),
        scratch_shapes=[
            pltpu.VMEM((2, s, LANES), BF16),
            pltpu.VMEM((2, s, LANES), BF16),
            pltpu.VMEM((2, nq, tq, tq), F32),
        ],
        compiler_params=_cparams(("parallel", "parallel", "arbitrary")),
        name="fox_attn",
    )(proj, proj, proj, cum)


def _diff_kernel(q_ref, k_ref, v_ref, lam_ref, g_ref, o_ref, ka_ref, va_ref, s_ref, *,
                 tq, slopes, out_scale, lambda_init):
    p = pl.program_id(1)
    i = pl.program_id(2)
    s_len = k_ref.shape[1]
    dq = DIFF_QK_DIM

    @pl.when(i == 0)
    def _prep():
        lane = lax.broadcasted_iota(jnp.int32, (s_len, LANES), 1)
        pos = lax.broadcasted_iota(jnp.int32, (s_len, LANES), 0).astype(F32)
        k = k_ref[0].astype(F32)
        v = v_ref[0].astype(F32)
        for h in range(2):
            slope = jnp.where(p == 0, slopes[h], slopes[2 + h]).astype(F32)
            pieces = _split3(pos * (slope * LOG2E))
            for c in range(2):
                m = 2 * h + c
                own = (lane >= m * dq) & (lane < (m + 1) * dq)
                spare = ((m + 1) % 4) * dq
                ka_ref[m] = _aug(k, own, lane, spare, pieces).astype(BF16)
            va_ref[h] = _v_aug(v, lane, h)

    lane_q = lax.broadcasted_iota(jnp.int32, (tq, LANES), 1)
    q = q_ref[0].astype(F32)
    qa = []
    for m in range(4):
        own = (lane_q >= m * dq) & (lane_q < (m + 1) * dq)
        spare = ((m + 1) % 4) * dq
        qa.append(jnp.where(own, q, _ones_at(lane_q, spare, 3)).astype(BF16))

    lp = lam_ref[...]
    lam = (jnp.exp(jnp.sum(lp[0:1] * lp[1:2], axis=1, keepdims=True))
           - jnp.exp(jnp.sum(lp[2:3] * lp[3:4], axis=1, keepdims=True)) + lambda_init)
    r = lax.broadcasted_iota(jnp.int32, (LANES, LANES), 0) // HEAD_DIM
    cc = lax.broadcasted_iota(jnp.int32, (LANES, LANES), 1) // HEAD_DIM
    same = jnp.where(r == cc, 1.0, 0.0).astype(BF16)

    def finish(accs):
        o1 = _normalize_pair(accs[0], accs[2], lane_q)
        o2 = _normalize_pair(accs[1], accs[3], lane_q)
        o = o1 - lam * o2
        o2_hi, o2_lo = _split2(o * o)
        ms = (_dot(o2_hi, same) + _dot(o2_lo, same)) * (1.0 / HEAD_DIM)
        y = o * lax.rsqrt(ms + RMS_EPS) * g_ref[...]
        o_ref[0] = (y * out_scale).astype(o_ref.dtype)

    _two_pass_attention(qa, ka_ref, va_ref, s_ref, (0, 0, 1, 1), i, tq, finish)


def _diff_attn(proj, lam_params, g2, tq, slopes, out_scale, lambda_init):
    b, s, _ = proj.shape
    nq = s // tq
    qc, kc, vc = OFF_QD // LANES, OFF_KD // LANES, OFF_VD // LANES
    return pl.pallas_call(
        functools.partial(_diff_kernel, tq=tq, slopes=slopes, out_scale=out_scale,
                          lambda_init=lambda_init),
        grid=(b, 2, nq),
        in_specs=[
            pl.BlockSpec((1, tq, LANES), lambda bi, p, i: (bi, i, qc + p)),
            pl.BlockSpec((1, s, LANES), lambda bi, p, i: (bi, 0, kc + p)),
            pl.BlockSpec((1, s, LANES), lambda bi, p, i: (bi, 0, vc + p)),
            pl.BlockSpec((4, DIFF_QK_DIM), lambda bi, p, i: (0, 0)),
            pl.BlockSpec((1, LANES), lambda bi, p, i: (0, 0)),
        ],
        out_specs=pl.BlockSpec((1, tq, LANES), lambda bi, p, i: (bi, i, p)),
        out_shape=jax.ShapeDtypeStruct((b, s, BW), BF16),
        scratch_shapes=[
            pltpu.VMEM((4, s, LANES), BF16),
            pltpu.VMEM((2, s, LANES), BF16),
            pltpu.VMEM((4, nq, tq, tq), F32),
        ],
        compiler_params=_cparams(("parallel", "parallel", "arbitrary")),
        name="diff_attn",
    )(proj, proj, proj, lam_params, g2)


def _dil_masks():
    t = np.arange(DB)[:, None]
    s = np.arange(DB)[None, :]
    first = np.where(s <= t, 0.0, NEG)
    s2 = np.arange(2 * DB)[None, :]
    dist = t + DB - s2
    band = np.where((dist >= 0) & (dist <= DB), 0.0, NEG)
    return first.astype(np.float32), band.astype(np.float32)


def _dil_bias_table(seq, slopes):
    out = np.zeros((N_GROUPS, N_HEADS, seq, LANES), np.float32)
    rho = np.arange(seq)
    for g, (window, dil) in enumerate(DIL_PATTERNS):
        cls_len = seq // dil
        pos = (rho % cls_len) * dil + rho // cls_len
        for hh in range(N_HEADS):
            spare = HEAD_DIM if hh % 2 == 0 else 0
            rest = pos.astype(np.float64) * (slopes[g][hh] * LOG2E)
            for i in range(N_PIECES):
                piece = rest.astype(ml_dtypes.bfloat16).astype(np.float64)
                out[g, hh, :, spare + i] = piece
                rest = rest - piece
    return out.astype(ml_dtypes.bfloat16)


def _dil_kernel(*refs, slopes):
    q_refs, k_refs, v_refs = refs[0:3], refs[3:6], refs[6:9]
    (bias_ref, first_ref, band_ref, o_ref, qa_ref, ka_ref, va_ref, og_ref, lse_ref,
     s_ref, m_ref, acc_ref) = refs[9:]
    p = pl.program_id(1)
    s_len = k_refs[0].shape[1]
    n_blk = s_len // DB

    def slope_of(g, h):
        return jnp.where(p == 0, slopes[g][h], slopes[g][2 + h]).astype(F32)

    lane1 = lax.broadcasted_iota(jnp.int32, (1, LANES), 1)
    for h in range(2):
        own_l = (lane1 < HEAD_DIM) if h == 0 else (lane1 >= HEAD_DIM)
        spare = HEAD_DIM if h == 0 else 0
        own = jnp.where(own_l, 1.0, 0.0).astype(BF16)
        other = jnp.where(own_l, 0.0, 1.0).astype(BF16)
        ones = _ones_at(lane1, spare, N_PIECES).astype(BF16)
        for g in range(N_GROUPS):
            qa_ref[2 * g + h] = q_refs[g][0] * own + ones
            ka_ref[2 * g + h] = k_refs[g][0] * own + bias_ref[g, h]
            va_ref[2 * g + h] = v_refs[g][0] * own + other

    low = lax.broadcasted_iota(jnp.int32, (s_len, LANES), 1) < HEAD_DIM
    for g, (window, dil) in enumerate(DIL_PATTERNS):
        cls_blocks = n_blk // dil
        cls_len = s_len // dil

        def window_of(j):
            first = j % cls_blocks == 0
            return (j * DB if first else (j - 1) * DB), (DB if first else 2 * DB), first

        for j in range(n_blk):
            k0, kw, first = window_of(j)
            mask = first_ref[...] if first else band_ref[...]
            for h in range(2):
                gh = 2 * g + h
                s = _dot_nt(qa_ref[gh, j * DB:(j + 1) * DB, :], ka_ref[gh, k0:k0 + kw, :]) + mask
                s_ref[h, j, :, :kw] = s
                m = jnp.max(s, axis=1, keepdims=True)
                m_ref[h, j * DB:(j + 1) * DB, :] = jnp.broadcast_to(m, (DB, LANES))
        for j in range(n_blk):
            k0, kw, first = window_of(j)
            for h in range(2):
                gh = 2 * g + h
                m = m_ref[h, j * DB:(j + 1) * DB, :]
                mb = m if first else jnp.concatenate([m, m], axis=1)
                pr = jnp.exp2(s_ref[h, j, :, :kw] - mb).astype(BF16)
                acc_ref[h, j * DB:(j + 1) * DB, :] = _dot(pr, va_ref[gh, k0:k0 + kw, :])
        acc0, acc1 = acc_ref[0], acc_ref[1]
        num = jnp.where(low, acc0, acc1)
        den = jnp.where(low, pltpu.roll(acc0, HEAD_DIM, 1), pltpu.roll(acc1, HEAD_DIM, 1))
        out_g = num / den
        lse_g = jnp.where(low, m_ref[0], m_ref[1]) + jnp.log(den) * LOG2E
        for c in range(dil):
            rows = pl.ds(c, cls_len, stride=dil) if dil > 1 else pl.ds(0, cls_len)
            og_ref[g, rows, :] = out_g[c * cls_len:(c + 1) * cls_len]
            lse_ref[g, rows, :] = lse_g[c * cls_len:(c + 1) * cls_len]

    lane_s = lax.broadcasted_iota(jnp.int32, (s_len, LANES), 1)
    t = lax.broadcasted_iota(jnp.int32, (s_len, LANES), 0).astype(F32)
    lses = []
    for g in range(N_GROUPS):
        slope_lane = jnp.where(lane_s < HEAD_DIM, slope_of(g, 0), slope_of(g, 1)) * LOG2E
        lses.append(lse_ref[g] - slope_lane * t)
    top = jnp.maximum(jnp.maximum(lses[0], lses[1]), lses[2])
    wsum = jnp.zeros((s_len, LANES), F32)
    out = jnp.zeros((s_len, LANES), F32)
    for g in range(N_GROUPS):
        w = jnp.exp2(lses[g] - top)
        wsum = wsum + w
        out = out + w * og_ref[g]
    o_ref[0] = (out / wsum).astype(o_ref.dtype)


def _dil_attn(proj, g1, g2, bias, first, band, slopes):
    b, s, _ = proj.shape
    per = BW // LANES

    def spec(blk):
        return pl.BlockSpec((1, s, LANES), lambda bi, p: (bi, 0, blk + p))

    arrays, in_specs = [], []
    for kind, off in enumerate((OFF_QB, OFF_KB, OFF_VB)):
        arrays += [proj, g1, g2]
        in_specs += [spec(off // LANES), spec(kind * per), spec(kind * per)]
    in_specs += [pl.BlockSpec((N_GROUPS, 2, s, LANES), lambda bi, p: (0, p, 0, 0)),
                 pl.BlockSpec(first.shape, lambda bi, p: (0, 0)),
                 pl.BlockSpec(band.shape, lambda bi, p: (0, 0))]
    return pl.pallas_call(
        functools.partial(_dil_kernel, slopes=slopes),
        grid=(b, 2),
        in_specs=in_specs,
        out_specs=pl.BlockSpec((1, s, LANES), lambda bi, p: (bi, 0, p)),
        out_shape=jax.ShapeDtypeStruct((b, s, BW), BF16),
        scratch_shapes=[
            pltpu.VMEM((2 * N_GROUPS, s, LANES), BF16),
            pltpu.VMEM((2 * N_GROUPS, s, LANES), BF16),
            pltpu.VMEM((2 * N_GROUPS, s, LANES), BF16),
            pltpu.VMEM((N_GROUPS, s, LANES), F32),
            pltpu.VMEM((N_GROUPS, s, LANES), F32),
            pltpu.VMEM((2, s // DB, DB, 2 * DB), F32),
            pltpu.VMEM((2, s, LANES), F32),
            pltpu.VMEM((2, s, LANES), F32),
        ],
        compiler_params=_cparams(("parallel", "parallel")),
        name="dil_attn",
    )(*arrays, bias, first, band)


def _sb_kernel(q_ref, k_ref, v_ref, o_ref, *, tq, tk):
    i = pl.program_id(2)
    n_diag = tq // tk
    lane_q = lax.broadcasted_iota(jnp.int32, (tq, LANES), 1)
    q = q_ref[0].astype(F32)
    row = lax.broadcasted_iota(jnp.int32, (tq, tk), 0)
    col = lax.broadcasted_iota(jnp.int32, (tq, tk), 1)
    tr = lax.broadcasted_iota(jnp.int32, (tk, tk), 0)
    tc = lax.broadcasted_iota(jnp.int32, (tk, tk), 1)
    tri = jnp.where(tr >= tc, 1.0, 0.0).astype(BF16)
    qn = [jnp.where((lane_q < HEAD_DIM) if h == 0 else (lane_q >= HEAD_DIM), -q, 0.0).astype(BF16)
          for h in range(2)]

    def scores(h, c, strict):
        kc = k_ref[0, c * tk:(c + 1) * tk, :]
        zn = _dot_nt(qn[h], kc)
        lk = jnp.minimum(zn, 0.0) - jnp.log(1.0 + jnp.exp(-jnp.abs(zn)))
        if strict is not None:
            lk = jnp.where(strict, lk, 0.0)
        return zn, _dot(lk.astype(BF16), tri)

    def weigh(h, c, strict, zn, suffix, run, acc):
        vc = v_ref[0, c * tk:(c + 1) * tk, :]
        run_b = jnp.concatenate([run] * (tk // LANES), axis=1)
        a = jnp.exp(suffix + run_b - zn)
        if strict is not None:
            a = jnp.where(strict, a, 0.0)
        acc = acc + _dot(a.astype(BF16), vc)
        return run + jnp.broadcast_to(suffix[:, 0:1], (tq, LANES)), acc

    def step(chunks, stricts, carry):
        parts = [[scores(h, c, st) for c, st in zip(chunks, stricts)] for h in range(2)]
        out = []
        for h in range(2):
            run, acc = carry[2 * h], carry[2 * h + 1]
            for c, st, (zn, suffix) in zip(chunks, stricts, parts[h]):
                run, acc = weigh(h, c, st, zn, suffix, run, acc)
            out += [run, acc]
        return tuple(out)

    def block(k):
        carry = tuple(jnp.zeros((tq, LANES), F32) for _ in range(4))
        diag = list(range(n_diag - 1, -1, -1))
        carry = step([k * n_diag + d for d in diag], [col + d * tk < row for d in diag], carry)
        for n in range(k):
            newest = k * n_diag - 1 - n * n_diag
            carry = step([newest - d for d in range(n_diag)], [None] * n_diag, carry)
        o_ref[0] = jnp.where(lane_q < HEAD_DIM, carry[1], carry[3]).astype(o_ref.dtype)

    for k in range(k_ref.shape[1] // tq):
        pl.when(i == k)(functools.partial(block, k))


def _sb_attn(proj, tq, tk):
    b, s, _ = proj.shape
    nq = s // tq
    qc, kc, vc = OFF_QA // LANES, OFF_KA // LANES, OFF_VA // LANES
    return pl.pallas_call(
        functools.partial(_sb_kernel, tq=tq, tk=tk),
        grid=(b, 2, nq),
        in_specs=[
            pl.BlockSpec((1, tq, LANES), lambda bi, p, i: (bi, i, qc + p)),
            pl.BlockSpec((1, s, LANES), lambda bi, p, i: (bi, 0, kc + p)),
            pl.BlockSpec((1, s, LANES), lambda bi, p, i: (bi, 0, vc + p)),
        ],
        out_specs=pl.BlockSpec((1, tq, LANES), lambda bi, p, i: (bi, i, p)),
        out_shape=jax.ShapeDtypeStruct((b, s, BW), BF16),
        compiler_params=_cparams(("parallel", "parallel", "arbitrary")),
        name="sb_attn",
    )(proj, proj, proj)


def _merge_kernel(x_ref, g_ref, ya_ref, yb_ref, yc_ref, yd_ref, wg_ref, wb_ref, wo_ref, o_ref):
    x = x_ref[...]
    h = _rms(x, g_ref[...]).astype(BF16)
    merged = None
    for n, y_ref in enumerate((ya_ref, yb_ref, yc_ref, yd_ref)):
        br = _dot(y_ref[...], wb_ref[n])
        gl = _dot(h, wg_ref[:, n * D_MODEL:(n + 1) * D_MODEL])
        term = br * (0.5 + 0.5 * jnp.tanh(0.5 * gl))
        merged = term if merged is None else merged + term
    o_ref[...] = x + _dot(merged.astype(BF16), wo_ref[...])


def _merge_out(x2d, g, ys, wg, wb, wo, tm):
    n, d = x2d.shape
    resident = pl.Buffered(1)
    return pl.pallas_call(
        _merge_kernel,
        grid=(n // tm,),
        in_specs=[pl.BlockSpec((tm, d), lambda i: (i, 0)),
                  pl.BlockSpec((1, d), lambda i: (0, 0))]
        + [pl.BlockSpec((tm, BW), lambda i: (i, 0))] * 4
        + [
            pl.BlockSpec((d, GATE_W), lambda i: (0, 0), pipeline_mode=resident),
            pl.BlockSpec((4, BW, d), lambda i: (0, 0, 0), pipeline_mode=resident),
            pl.BlockSpec((d, d), lambda i: (0, 0), pipeline_mode=resident),
        ],
        out_specs=pl.BlockSpec((tm, d), lambda i: (i, 0)),
        out_shape=jax.ShapeDtypeStruct((n, d), F32),
        compiler_params=_cparams(("parallel",)),
        name="merge_out",
    )(x2d, g, *ys, wg, wb, wo)


def _mlp_kernel(x_ref, g_ref, wu_ref, wd_ref, gf_ref, o_ref, *, final_norm, tf):
    x = x_ref[...]
    h = _rms(x, g_ref[...]).astype(BF16)
    y = x
    for c in range(wu_ref.shape[1] // tf):
        up = _dot(h, wu_ref[:, c * tf:(c + 1) * tf])
        act = jnp.square(jnp.maximum(up, 0.0)).astype(BF16)
        y = y + _dot(act, wd_ref[c * tf:(c + 1) * tf, :])
    if final_norm:
        y = _rms(y, gf_ref[...])
    o_ref[...] = y


def _mlp(x2d, g, wu, wd, gf, tm, tf, final_norm):
    n, d = x2d.shape
    dff = wu.shape[1]
    resident = pl.Buffered(1)
    return pl.pallas_call(
        functools.partial(_mlp_kernel, final_norm=final_norm, tf=tf),
        grid=(n // tm,),
        in_specs=[
            pl.BlockSpec((tm, d), lambda i: (i, 0)),
            pl.BlockSpec((1, d), lambda i: (0, 0)),
            pl.BlockSpec((d, dff), lambda i: (0, 0), pipeline_mode=resident),
            pl.BlockSpec((dff, d), lambda i: (0, 0), pipeline_mode=resident),
            pl.BlockSpec((1, d), lambda i: (0, 0)),
        ],
        out_specs=pl.BlockSpec((tm, d), lambda i: (i, 0)),
        out_shape=jax.ShapeDtypeStruct((n, d), F32),
        compiler_params=_cparams(("parallel",)),
        name="mlp",
    )(x2d, g, wu, wd, gf)


def _col_scale():
    cs = np.ones((1, W_MAIN), np.float32)
    cs[0, OFF_QA:OFF_QA + BW] = HEAD_DIM ** -0.5
    cs[0, OFF_QB:OFF_QB + N_GROUPS * BW] = HEAD_DIM ** -0.5 * LOG2E
    cs[0, OFF_QC:OFF_QC + BW] = HEAD_DIM ** -0.5 * LOG2E
    cs[0, OFF_QD:OFF_QD + BW] = DIFF_QK_DIM ** -0.5 * LOG2E
    return cs


def kernel(x, mix_norm_g, w_in, b_forget, lambda_q1, lambda_k1, lambda_q2, lambda_k2,
           diff_norm_g, w_branch, w_out, mlp_norm_g, w_up, w_down, final_norm_g):
    bsz, seq, d = x.shape
    depth = w_in.shape[0]
    n = bsz * seq
    tq_big = min(512, seq)
    tm = min(1024, n)

    slopes = [float(v) for v in _alibi_slopes()]
    n0, n1 = N_HEADS, 2 * N_HEADS
    slopes_dil = (tuple(slopes[:n0]), tuple(slopes[n1:n1 + N_HEADS]), tuple(slopes[n1 + N_HEADS:]))
    slopes_diff = tuple(slopes[n0:n1])

    assert all(w // r == DB and seq % (DB * r) == 0 for w, r in DIL_PATTERNS) and n % tm == 0
    first, band = (jnp.asarray(m) for m in _dil_masks())
    dil_bias = jnp.asarray(_dil_bias_table(seq, slopes_dil))
    tril = jnp.asarray(np.tril(np.ones((2 * LANES, 2 * LANES), np.float32)), BF16)
    colscale = jnp.asarray(_col_scale())

    x2d = x.reshape(n, d)
    for l in range(depth):
        lambda_init = 0.8 - 0.6 * math.exp(-0.3 * l)
        wl = w_in[l]
        d_end = F_COL + N_HEADS + 3 * BW
        w_a = wl[:, :F_COL].astype(BF16)
        w_d = wl[:, F_COL + N_HEADS:d_end].astype(BF16)
        w_g = wl[:, d_end:].astype(BF16)
        wf = jnp.pad(wl[:, F_COL:F_COL + N_HEADS], ((0, 0), (0, LANES - N_HEADS)))
        wf_hi = wf.astype(BF16)
        wf_lo = (wf - wf_hi.astype(F32)).astype(BF16)
        bf = jnp.pad(b_forget[l].astype(F32), (0, LANES - N_HEADS)).reshape(1, LANES)
        g_mix = mix_norm_g[l].reshape(1, d)

        proj, pg1, pg2 = _norm_proj(x2d, g_mix, w_a, w_d, colscale, bsz, min(512, seq))
        cum = _forget_cum(x2d, g_mix, wf_hi, wf_lo, bf, tril, bsz, tm)
        proj3 = proj.reshape(bsz, seq, W_MAIN)

        y_a = _sb_attn(proj3, tq_big, min(256, seq))
        y_b = _dil_attn(proj3, pg1.reshape(bsz, seq, -1), pg2.reshape(bsz, seq, -1), dil_bias,
                        first, band, slopes_dil)
        y_c = _fox_attn(proj3, cum, tq_big)
        lam_params = jnp.stack([lambda_q1[l], lambda_k1[l], lambda_q2[l], lambda_k2[l]]).astype(F32)
        g2 = jnp.tile(diff_norm_g[l].astype(F32), 2).reshape(1, LANES)
        y_d = _diff_attn(proj3, lam_params, g2, tq_big, slopes_diff, 1.0 - lambda_init, lambda_init)

        ys = [y.reshape(n, BW) for y in (y_a, y_b, y_c, y_d)]
        x2d = _merge_out(x2d, g_mix, ys, w_g, w_branch[l].astype(BF16), w_out[l].astype(BF16),
                         min(512, n))
        x2d = _mlp(x2d, mlp_norm_g[l].reshape(1, d), w_up[l].astype(BF16),
                   w_down[l].astype(BF16), final_norm_g.reshape(1, d), tm, 1024,
                   final_norm=(l == depth - 1))
    return x2d.reshape(bsz, seq, d)
```

```python
import functools
import math

import ml_dtypes
import numpy as np
import jax
import jax.numpy as jnp
from jax import lax
from jax.experimental import pallas as pl
from jax.experimental.pallas import tpu as pltpu

F32 = jnp.float32
BF16 = jnp.bfloat16

D_MODEL = 1024
HEAD_DIM = 64
N_HEADS = 4
DIL_PATTERNS = ((128, 1), (512, 4), (2048, 16))
N_GROUPS = len(DIL_PATTERNS)
DIFF_QK_DIM = HEAD_DIM // 2
D_FF = 4 * D_MODEL
RMS_EPS = 1e-6
N_ALIBI = N_GROUPS * N_HEADS + N_HEADS

LANES = 128
BW = N_HEADS * HEAD_DIM
NEG = -1e30
LOG2E = math.log2(math.e)
DB = 128
VMEM_LIMIT = 56 * 1024 * 1024

OFF_QA, OFF_KA, OFF_VA = 0, BW, 2 * BW
OFF_QB, OFF_KB, OFF_VB = 3 * BW, 6 * BW, 9 * BW
OFF_QC, OFF_KC, OFF_VC = 12 * BW, 13 * BW, 14 * BW
OFF_QD, OFF_KD, OFF_VD = 15 * BW, 16 * BW, 17 * BW
W_MAIN = 18 * BW
GATE_W = 4 * D_MODEL
F_COL = 15 * BW
N_PIECES = 4


def _alibi_slopes():
    return 2.0 ** (-8.0 * np.arange(1, N_ALIBI + 1) / N_ALIBI)


def _cparams(sem):
    return pltpu.CompilerParams(dimension_semantics=sem, vmem_limit_bytes=VMEM_LIMIT)


def _rms(x, g):
    ms = jnp.mean(x * x, axis=-1, keepdims=True)
    return x * lax.rsqrt(ms + RMS_EPS) * g


def _split3(val):
    p1 = val.astype(BF16).astype(F32)
    r1 = val - p1
    p2 = r1.astype(BF16).astype(F32)
    p3 = (r1 - p2).astype(BF16).astype(F32)
    return p1, p2, p3


def _split2(val):
    hi = val.astype(BF16)
    lo = (val - hi.astype(F32)).astype(BF16)
    return hi, lo


def _dot_nt(a, b):
    return lax.dot_general(a, b, (((1,), (1,)), ((), ())), preferred_element_type=F32)


def _dot(a, b):
    return jnp.dot(a, b, preferred_element_type=F32)


def _fold_max(s):
    m = s[:, :LANES]
    for j in range(1, s.shape[1] // LANES):
        m = jnp.maximum(m, s[:, j * LANES:(j + 1) * LANES])
    return m


def _cast_kernel(x_ref, o_ref):
    o_ref[...] = x_ref[...].astype(o_ref.dtype)


def _cast_bf16(w2d, tr):
    r, c = w2d.shape
    return pl.pallas_call(
        _cast_kernel,
        grid=(r // tr,),
        in_specs=[pl.BlockSpec((tr, c), lambda i: (i, 0))],
        out_specs=pl.BlockSpec((tr, c), lambda i: (i, 0)),
        out_shape=jax.ShapeDtypeStruct((r, c), BF16),
        compiler_params=_cparams(("parallel",)),
        name="cast_bf16",
    )(w2d)


def _split_w_in_kernel(w_ref, a_ref, d_ref, g_ref, fhi_ref, flo_ref):
    rows = w_ref.shape[1]
    a_ref[0] = w_ref[0, :, :F_COL].astype(BF16)
    d_w, g_w = d_ref.shape[2], g_ref.shape[2]
    d_sup = w_ref[0, :, F_COL:F_COL + d_w + LANES]
    d_ref[0] = pltpu.roll(d_sup, d_w + LANES - N_HEADS, 1)[:, :d_w].astype(BF16)
    g0 = F_COL + d_w
    tail = w_ref[0, :, g0 + g_w:g0 + g_w + N_HEADS]
    g_sup = jnp.concatenate([w_ref[0, :, g0:g0 + g_w], tail,
                             jnp.zeros((rows, LANES - N_HEADS), F32)], axis=1)
    g_ref[0] = pltpu.roll(g_sup, g_w + LANES - N_HEADS, 1)[:, :g_w].astype(BF16)
    lane = lax.broadcasted_iota(jnp.int32, (rows, LANES), 1)
    f = jnp.where(lane < N_HEADS, w_ref[0, :, F_COL:F_COL + LANES], 0.0)
    f_hi = f.astype(BF16)
    fhi_ref[0] = f_hi
    flo_ref[0] = (f - f_hi.astype(F32)).astype(BF16)


def _split_w_in(w_in, tr):
    depth, d, d_in = w_in.shape
    d_w, g_w = 3 * BW, GATE_W
    assert d_in == F_COL + N_HEADS + d_w + g_w
    widths = (F_COL, d_w, g_w, LANES, LANES)
    return pl.pallas_call(
        _split_w_in_kernel,
        grid=(depth, d // tr),
        in_specs=[pl.BlockSpec((1, tr, d_in), lambda l, i: (l, i, 0))],
        out_specs=[pl.BlockSpec((1, tr, w), lambda l, i: (l, i, 0)) for w in widths],
        out_shape=[jax.ShapeDtypeStruct((depth, d, w), BF16) for w in widths],
        compiler_params=_cparams(("parallel", "parallel")),
        name="split_w_in",
    )(w_in)


def _norm_proj_kernel(x_ref, g_ref, wa_ref, wd_ref, cs_ref, fhi_ref, flo_ref, fb_ref,
                      o_ref, g1_ref, g2_ref, logf_ref, stage_ref, *, tn):
    tm = x_ref.shape[0]
    h32 = _rms(x_ref[...], g_ref[...])
    h, h_lo = _split2(h32)
    y = (_dot(h, fhi_ref[...]) + _dot(h, flo_ref[...]) + _dot(h_lo, fhi_ref[...])) + fb_ref[...]
    logf_ref[...] = jnp.minimum(y, 0.0) - jnp.log(1.0 + jnp.exp(-jnp.abs(y)))
    na = wa_ref.shape[1] // tn
    for c in range(na + wd_ref.shape[1] // tn):
        w = wa_ref[:, c * tn:(c + 1) * tn] if c < na else wd_ref[:, (c - na) * tn:(c - na + 1) * tn]
        cols = slice(c * tn, (c + 1) * tn)
        res = _dot(h, w) * cs_ref[:, cols]
        o_ref[:, cols] = res.astype(o_ref.dtype)
        kind = c - OFF_QB // tn
        if 0 <= kind < 3:
            per = BW // LANES
            for j in range((N_GROUPS - 1) * per):
                stage_ref[j] = res[:, BW + j * LANES:BW + (j + 1) * LANES]
            for gi, out_ref in enumerate((g1_ref, g2_ref)):
                dil = DIL_PATTERNS[gi + 1][1]
                for cl in range(dil):
                    for j in range(per):
                        rows = stage_ref[gi * per + j, pl.ds(cl, tm // dil, stride=dil), :]
                        lanes = slice(kind * BW + j * LANES, kind * BW + (j + 1) * LANES)
                        out_ref[0, cl, :, lanes] = rows.astype(out_ref.dtype)


def _norm_proj(x2d, g, w_a, w_d, wf_hi, wf_lo, layer, colscale, bf, bsz, tm):
    n, d = x2d.shape
    s = n // bsz
    per_batch = s // tm
    tn = N_GROUPS * BW
    resident = pl.Buffered(1)
    d1, d2 = DIL_PATTERNS[1][1], DIL_PATTERNS[2][1]
    return pl.pallas_call(
        functools.partial(_norm_proj_kernel, tn=tn),
        grid=(n // tm,),
        in_specs=[
            pl.BlockSpec((tm, d), lambda i: (i, 0)),
            pl.BlockSpec((1, d), lambda i: (0, 0)),
            pl.BlockSpec((None,) + w_a.shape[1:], lambda i: (layer, 0, 0), pipeline_mode=resident),
            pl.BlockSpec((None,) + w_d.shape[1:], lambda i: (layer, 0, 0), pipeline_mode=resident),
            pl.BlockSpec((1, W_MAIN), lambda i: (0, 0)),
            pl.BlockSpec((None, d, LANES), lambda i: (layer, 0, 0)),
            pl.BlockSpec((None, d, LANES), lambda i: (layer, 0, 0)),
            pl.BlockSpec((1, LANES), lambda i: (0, 0)),
        ],
        out_specs=[
            pl.BlockSpec((tm, W_MAIN), lambda i: (i, 0)),
            pl.BlockSpec((1, d1, tm // d1, tn), lambda i: (i // per_batch, 0, i % per_batch, 0)),
            pl.BlockSpec((1, d2, tm // d2, tn), lambda i: (i // per_batch, 0, i % per_batch, 0)),
            pl.BlockSpec((tm, LANES), lambda i: (i, 0)),
        ],
        out_shape=[
            jax.ShapeDtypeStruct((n, W_MAIN), BF16),
            jax.ShapeDtypeStruct((bsz, d1, s // d1, tn), BF16),
            jax.ShapeDtypeStruct((bsz, d2, s // d2, tn), BF16),
            jax.ShapeDtypeStruct((n, LANES), F32),
        ],
        scratch_shapes=[pltpu.VMEM(((N_GROUPS - 1) * BW // LANES, tm, LANES), F32)],
        compiler_params=_cparams(("parallel",)),
        name="norm_proj",
    )(x2d, g, w_a, w_d, colscale, wf_hi, wf_lo, bf)


def _cumsum_kernel(l_ref, tril_ref, o_ref):
    tril = tril_ref[...]
    blk = tril.shape[0]
    carry = jnp.zeros((1, LANES), F32)
    for j in range(l_ref.shape[1] // blk):
        l1, l2, l3 = _split3(l_ref[0, j * blk:(j + 1) * blk, :])
        c = (_dot(tril, l1.astype(BF16)) + _dot(tril, l2.astype(BF16))
             + _dot(tril, l3.astype(BF16))) + carry
        o_ref[0, j * blk:(j + 1) * blk, :] = c
        carry = c[blk - 1:blk, :]


def _forget_cum(logf, tril, bsz):
    s = logf.shape[0] // bsz
    return pl.pallas_call(
        _cumsum_kernel,
        grid=(bsz,),
        in_specs=[
            pl.BlockSpec((1, s, LANES), lambda i: (i, 0, 0)),
            pl.BlockSpec(tril.shape, lambda i: (0, 0)),
        ],
        out_specs=pl.BlockSpec((1, s, LANES), lambda i: (i, 0, 0)),
        out_shape=jax.ShapeDtypeStruct((bsz, s, LANES), F32),
        compiler_params=_cparams(("parallel",)),
        name="forget_cumsum",
    )(logf.reshape(bsz, s, LANES), tril)


def _aug(base, keep, lane, lane0, pieces):
    out = jnp.where(keep, base, 0.0)
    for i, p in enumerate(pieces):
        out = jnp.where(lane == lane0 + i, p, out)
    return out


def _ones_at(lane, lane0, n):
    return jnp.where((lane >= lane0) & (lane < lane0 + n), 1.0, 0.0)


def _v_aug(v, lane, h):
    own = (lane < HEAD_DIM) if h == 0 else (lane >= HEAD_DIM)
    return jnp.where(own, v, 1.0).astype(BF16)


def _normalize_pair(acc0, acc1, lane):
    den0 = pltpu.roll(acc0, HEAD_DIM, 1)
    den1 = pltpu.roll(acc1, HEAD_DIM, 1)
    return jnp.where(lane < HEAD_DIM, acc0 / den0, acc1 / den1)


def _two_pass_attention(qa, ka_ref, va_ref, s_ref, v_of_map, i, tq, finish):
    n_maps = len(qa)
    tk = tq
    nq = ka_ref.shape[1] // tk
    row = lax.broadcasted_iota(jnp.int32, (tq, tk), 0)
    col = lax.broadcasted_iota(jnp.int32, (tq, tk), 1)
    causal = col <= row

    def block(k):
        mcol = [None] * n_maps
        for c in range(k + 1):
            for m in range(n_maps):
                s = _dot_nt(qa[m], ka_ref[m, c * tk:(c + 1) * tk, :])
                if c == k:
                    s = jnp.where(causal, s, NEG)
                s_ref[m, c] = s
                fm = _fold_max(s)
                mcol[m] = fm if mcol[m] is None else jnp.maximum(mcol[m], fm)
        accs = []
        for m in range(n_maps):
            mrow = jnp.broadcast_to(jnp.max(mcol[m], axis=1, keepdims=True), (tq, LANES))
            mb = jnp.concatenate([mrow] * (tk // LANES), axis=1)
            acc = None
            for c in range(k + 1):
                pr = jnp.exp2(s_ref[m, c] - mb).astype(BF16)
                part = _dot(pr, va_ref[v_of_map[m], c * tk:(c + 1) * tk, :])
                acc = part if acc is None else acc + part
            accs.append(acc)
        finish(accs)

    for k in range(nq):
        pl.when(i == k)(functools.partial(block, k))


def _fox_kernel(q_ref, k_ref, v_ref, cum_ref, o_ref, ka_ref, va_ref, s_ref, *, tq):
    p = pl.program_id(1)
    i = pl.program_id(2)
    s_len = k_ref.shape[1]

    @pl.when(i == 0)
    def _prep():
        lane = lax.broadcasted_iota(jnp.int32, (s_len, LANES), 1)
        k = k_ref[0].astype(F32)
        v = v_ref[0].astype(F32)
        cum = cum_ref[0]
        for h in range(2):
            own = (lane < HEAD_DIM) if h == 0 else (lane >= HEAD_DIM)
            spare = HEAD_DIM if h == 0 else 0
            c = jnp.sum(jnp.where(lane == 2 * p + h, cum, 0.0), axis=1, keepdims=True)
            pieces = _split3(jnp.broadcast_to(c * (-LOG2E), (s_len, LANES)))
            ka_ref[h] = _aug(k, own, lane, spare, pieces).astype(BF16)
            va_ref[h] = _v_aug(v, lane, h)

    lane_q = lax.broadcasted_iota(jnp.int32, (tq, LANES), 1)
    q = q_ref[0].astype(F32)
    qa = []
    for h in range(2):
        own = (lane_q < HEAD_DIM) if h == 0 else (lane_q >= HEAD_DIM)
        spare = HEAD_DIM if h == 0 else 0
        qa.append(jnp.where(own, q, _ones_at(lane_q, spare, 3)).astype(BF16))

    def finish(accs):
        o_ref[0] = _normalize_pair(accs[0], accs[1], lane_q).astype(o_ref.dtype)

    _two_pass_attention(qa, ka_ref, va_ref, s_ref, (0, 1), i, tq, finish)


def _fox_attn(proj, cum, tq):
    b, s, _ = proj.shape
    nq = s // tq
    qc, kc, vc = OFF_QC // LANES, OFF_KC // LANES, OFF_VC // LANES
    return pl.pallas_call(
        functools.partial(_fox_kernel, tq=tq),
        grid=(b, 2, nq),
        in_specs=[
            pl.BlockSpec((1, tq, LANES), lambda bi, p, i: (bi, i, qc + p)),
            pl.BlockSpec((1, s, LANES), lambda bi, p, i: (bi, 0, kc + p)),
            pl.BlockSpec((1, s, LANES), lambda bi, p, i: (bi, 0, vc + p)),
            pl.BlockSpec((1, s, LANES), lambda bi, p, i: (bi, 0, 0)),
        ],
        out_specs=pl.BlockSpec((1, tq, LANES), lambda bi, p, i: (bi, i, p)),
        out_shape=jax.ShapeDtypeStruct((b, s, BW), BF16),
        scratch_shapes=[
            pltpu.VMEM((2, s, LANES), BF16),
            pltpu.VMEM((2, s, LANES), BF16),
            pltpu.VMEM((2, nq, tq, tq), F32),
        ],
        compiler_params=_cparams(("parallel", "parallel", "arbitrary")),
        name="fox_attn",
    )(proj, proj, proj, cum)


def _diff_kernel(q_ref, k_ref, v_ref, lam_ref, g_ref, o_ref, ka_ref, va_ref, s_ref, *,
                 tq, slopes, out_scale, lambda_init):
    p = pl.program_id(1)
    i = pl.program_id(2)
    s_len = k_ref.shape[1]
    dq = DIFF_QK_DIM

    @pl.when(i == 0)
    def _prep():
        lane = lax.broadcasted_iota(jnp.int32, (s_len, LANES), 1)
        pos = lax.broadcasted_iota(jnp.int32, (s_len, LANES), 0).astype(F32)
        k = k_ref[0].astype(F32)
        v = v_ref[0].astype(F32)
        for h in range(2):
            slope = jnp.where(p == 0, slopes[h], slopes[2 + h]).astype(F32)
            pieces = _split3(pos * (slope * LOG2E))
            for c in range(2):
                m = 2 * h + c
                own = (lane >= m * dq) & (lane < (m + 1) * dq)
                spare = ((m + 1) % 4) * dq
                ka_ref[m] = _aug(k, own, lane, spare, pieces).astype(BF16)
            va_ref[h] = _v_aug(v, lane, h)

    lane_q = lax.broadcasted_iota(jnp.int32, (tq, LANES), 1)
    q = q_ref[0].astype(F32)
    qa = []
    for m in range(4):
        own = (lane_q >= m * dq) & (lane_q < (m + 1) * dq)
        spare = ((m + 1) % 4) * dq
        qa.append(jnp.where(own, q, _ones_at(lane_q, spare, 3)).astype(BF16))

    lp = lam_ref[...]
    lam = (jnp.exp(jnp.sum(lp[0:1] * lp[1:2], axis=1, keepdims=True))
           - jnp.exp(jnp.sum(lp[2:3] * lp[3:4], axis=1, keepdims=True)) + lambda_init)
    r = lax.broadcasted_iota(jnp.int32, (LANES, LANES), 0) // HEAD_DIM
    cc = lax.broadcasted_iota(jnp.int32, (LANES, LANES), 1) // HEAD_DIM
    same = jnp.where(r == cc, 1.0, 0.0).astype(BF16)

    def finish(accs):
        o1 = _normalize_pair(accs[0], accs[2], lane_q)
        o2 = _normalize_pair(accs[1], accs[3], lane_q)
        o = o1 - lam * o2
        o2_hi, o2_lo = _split2(o * o)
        ms = (_dot(o2_hi, same) + _dot(o2_lo, same)) * (1.0 / HEAD_DIM)
        y = o * lax.rsqrt(ms + RMS_EPS) * g_ref[...]
        o_ref[0] = (y * out_scale).astype(o_ref.dtype)

    _two_pass_attention(qa, ka_ref, va_ref, s_ref, (0, 0, 1, 1), i, tq, finish)


def _diff_attn(proj, lam_params, g2, tq, slopes, out_scale, lambda_init):
    b, s, _ = proj.shape
    nq = s // tq
    qc, kc, vc = OFF_QD // LANES, OFF_KD // LANES, OFF_VD // LANES
    return pl.pallas_call(
        functools.partial(_diff_kernel, tq=tq, slopes=slopes, out_scale=out_scale,
                          lambda_init=lambda_init),
        grid=(b, 2, nq),
        in_specs=[
            pl.BlockSpec((1, tq, LANES), lambda bi, p, i: (bi, i, qc + p)),
            pl.BlockSpec((1, s, LANES), lambda bi, p, i: (bi, 0, kc + p)),
            pl.BlockSpec((1, s, LANES), lambda bi, p, i: (bi, 0, vc + p)),
            pl.BlockSpec((4, DIFF_QK_DIM), lambda bi, p, i: (0, 0)),
            pl.BlockSpec((1, LANES), lambda bi, p, i: (0, 0)),
        ],
        out_specs=pl.BlockSpec((1, tq, LANES), lambda bi, p, i: (bi, i, p)),
        out_shape=jax.ShapeDtypeStruct((b, s, BW), BF16),
        scratch_shapes=[
            pltpu.VMEM((4, s, LANES), BF16),
            pltpu.VMEM((2, s, LANES), BF16),
            pltpu.VMEM((4, nq, tq, tq), F32),
        ],
        compiler_params=_cparams(("parallel", "parallel", "arbitrary")),
        name="diff_attn",
    )(proj, proj, proj, lam_params, g2)


def _dil_masks():
    t = np.arange(DB)[:, None]
    s = np.arange(DB)[None, :]
    first = np.where(s <= t, 0.0, NEG)
    s2 = np.arange(2 * DB)[None, :]
    dist = t + DB - s2
    band = np.where((dist >= 0) & (dist <= DB), 0.0, NEG)
    return first.astype(np.float32), band.astype(np.float32)


def _dil_bias_table(seq, slopes):
    out = np.zeros((N_GROUPS, N_HEADS, seq, LANES), np.float32)
    rho = np.arange(seq)
    for g, (window, dil) in enumerate(DIL_PATTERNS):
        cls_len = seq // dil
        pos = (rho % cls_len) * dil + rho // cls_len
        for hh in range(N_HEADS):
            spare = HEAD_DIM if hh % 2 == 0 else 0
            rest = pos.astype(np.float64) * (slopes[g][hh] * LOG2E)
            for i in range(N_PIECES):
                piece = rest.astype(ml_dtypes.bfloat16).astype(np.float64)
                out[g, hh, :, spare + i] = piece
                rest = rest - piece
    return out.astype(ml_dtypes.bfloat16)


def _dil_kernel(*refs, slopes):
    q_refs, k_refs, v_refs = refs[0:3], refs[3:6], refs[6:9]
    (bias_ref, first_ref, band_ref, o_ref, qa_ref, ka_ref, va_ref, og_ref, lse_ref,
     s_ref, m_ref, acc_ref) = refs[9:]
    p = pl.program_id(1)
    s_len = k_refs[0].shape[1]
    n_blk = s_len // DB

    def slope_of(g, h):
        return jnp.where(p == 0, slopes[g][h], slopes[g][2 + h]).astype(F32)

    lane1 = lax.broadcasted_iota(jnp.int32, (1, LANES), 1)
    for h in range(2):
        own_l = (lane1 < HEAD_DIM) if h == 0 else (lane1 >= HEAD_DIM)
        spare = HEAD_DIM if h == 0 else 0
        own = jnp.where(own_l, 1.0, 0.0).astype(BF16)
        other = jnp.where(own_l, 0.0, 1.0).astype(BF16)
        ones = _ones_at(lane1, spare, N_PIECES).astype(BF16)
        for g in range(N_GROUPS):
            qa_ref[2 * g + h] = q_refs[g][0] * own + ones
            ka_ref[2 * g + h] = k_refs[g][0] * own + bias_ref[g, h]
            va_ref[2 * g + h] = v_refs[g][0] * own + other

    low = lax.broadcasted_iota(jnp.int32, (s_len, LANES), 1) < HEAD_DIM
    for g, (window, dil) in enumerate(DIL_PATTERNS):
        cls_blocks = n_blk // dil
        cls_len = s_len // dil

        def window_of(j):
            first = j % cls_blocks == 0
            return (j * DB if first else (j - 1) * DB), (DB if first else 2 * DB), first

        for j in range(n_blk):
            k0, kw, first = window_of(j)
            mask = first_ref[...] if first else band_ref[...]
            for h in range(2):
                gh = 2 * g + h
                s = _dot_nt(qa_ref[gh, j * DB:(j + 1) * DB, :], ka_ref[gh, k0:k0 + kw, :]) + mask
                s_ref[h, j, :, :kw] = s
                m = jnp.max(s, axis=1, keepdims=True)
                m_ref[h, j * DB:(j + 1) * DB, :] = jnp.broadcast_to(m, (DB, LANES))
        for j in range(n_blk):
            k0, kw, first = window_of(j)
            for h in range(2):
                gh = 2 * g + h
                m = m_ref[h, j * DB:(j + 1) * DB, :]
                mb = m if first else jnp.concatenate([m, m], axis=1)
                pr = jnp.exp2(s_ref[h, j, :, :kw] - mb).astype(BF16)
                acc_ref[h, j * DB:(j + 1) * DB, :] = _dot(pr, va_ref[gh, k0:k0 + kw, :])
        acc0, acc1 = acc_ref[0], acc_ref[1]
        num = jnp.where(low, acc0, acc1)
        den = jnp.where(low, pltpu.roll(acc0, HEAD_DIM, 1), pltpu.roll(acc1, HEAD_DIM, 1))
        out_g = num / den
        lse_g = jnp.where(low, m_ref[0], m_ref[1]) + jnp.log(den) * LOG2E
        for c in range(dil):
            rows = pl.ds(c, cls_len, stride=dil) if dil > 1 else pl.ds(0, cls_len)
            og_ref[g, rows, :] = out_g[c * cls_len:(c + 1) * cls_len]
            lse_ref[g, rows, :] = lse_g[c * cls_len:(c + 1) * cls_len]

    lane_s = lax.broadcasted_iota(jnp.int32, (s_len, LANES), 1)
    t = lax.broadcasted_iota(jnp.int32, (s_len, LANES), 0).astype(F32)
    lses = []
    for g in range(N_GROUPS):
        slope_lane = jnp.where(lane_s < HEAD_DIM, slope_of(g, 0), slope_of(g, 1)) * LOG2E
        lses.append(lse_ref[g] - slope_lane * t)
    top = jnp.maximum(jnp.maximum(lses[0], lses[1]), lses[2])
    wsum = jnp.zeros((s_len, LANES), F32)
    out = jnp.zeros((s_len, LANES), F32)
    for g in range(N_GROUPS):
        w = jnp.exp2(lses[g] - top)
        wsum = wsum + w
        out = out + w * og_ref[g]
    o_ref[0] = (out / wsum).astype(o_ref.dtype)


def _dil_attn(proj, g1, g2, bias, first, band, slopes):
    b, s, _ = proj.shape
    per = BW // LANES

    def spec(blk):
        return pl.BlockSpec((1, s, LANES), lambda bi, p: (bi, 0, blk + p))

    arrays, in_specs = [], []
    for kind, off in enumerate((OFF_QB, OFF_KB, OFF_VB)):
        arrays += [proj, g1, g2]
        in_specs += [spec(off // LANES), spec(kind * per), spec(kind * per)]
    in_specs += [pl.BlockSpec((N_GROUPS, 2, s, LANES), lambda bi, p: (0, p, 0, 0)),
                 pl.BlockSpec(first.shape, lambda bi, p: (0, 0)),
                 pl.BlockSpec(band.shape, lambda bi, p: (0, 0))]
    return pl.pallas_call(
        functools.partial(_dil_kernel, slopes=slopes),
        grid=(b, 2),
        in_specs=in_specs,
        out_specs=pl.BlockSpec((1, s, LANES), lambda bi, p: (bi, 0, p)),
        out_shape=jax.ShapeDtypeStruct((b, s, BW), BF16),
        scratch_shapes=[
            pltpu.VMEM((2 * N_GROUPS, s, LANES), BF16),
            pltpu.VMEM((2 * N_GROUPS, s, LANES), BF16),
            pltpu.VMEM((2 * N_GROUPS, s, LANES), BF16),
            pltpu.VMEM((N_GROUPS, s, LANES), F32),
            pltpu.VMEM((N_GROUPS, s, LANES), F32),
            pltpu.VMEM((2, s // DB, DB, 2 * DB), F32),
            pltpu.VMEM((2, s, LANES), F32),
            pltpu.VMEM((2, s, LANES), F32),
        ],
        compiler_params=_cparams(("parallel", "parallel")),
        name="dil_attn",
    )(*arrays, bias, first, band)


def _sb_kernel(q_ref, k_ref, v_ref, o_ref, *, tq, tk):
    i = pl.program_id(2)
    n_diag = tq // tk
    lane_q = lax.broadcasted_iota(jnp.int32, (tq, LANES), 1)
    q = q_ref[0].astype(F32)
    row = lax.broadcasted_iota(jnp.int32, (tq, tk), 0)
    col = lax.broadcasted_iota(jnp.int32, (tq, tk), 1)
    tr = lax.broadcasted_iota(jnp.int32, (tk, tk), 0)
    tc = lax.broadcasted_iota(jnp.int32, (tk, tk), 1)
    tri = jnp.where(tr >= tc, 1.0, 0.0).astype(BF16)
    qn = [jnp.where((lane_q < HEAD_DIM) if h == 0 else (lane_q >= HEAD_DIM), -q, 0.0).astype(BF16)
          for h in range(2)]

    def scores(h, c, strict):
        kc = k_ref[0, c * tk:(c + 1) * tk, :]
        zn = _dot_nt(qn[h], kc)
        lk = jnp.minimum(zn, 0.0) - jnp.log(1.0 + jnp.exp(-jnp.abs(zn)))
        if strict is not None:
            lk = jnp.where(strict, lk, 0.0)
        return zn, _dot(lk.astype(BF16), tri)

    def weigh(h, c, strict, zn, suffix, run, acc):
        vc = v_ref[0, c * tk:(c + 1) * tk, :]
        run_b = jnp.concatenate([run] * (tk // LANES), axis=1)
        a = jnp.exp(suffix + run_b - zn)
        if strict is not None:
            a = jnp.where(strict, a, 0.0)
        acc = acc + _dot(a.astype(BF16), vc)
        return run + jnp.broadcast_to(suffix[:, 0:1], (tq, LANES)), acc

    def step(chunks, stricts, carry):
        parts = [[scores(h, c, st) for c, st in zip(chunks, stricts)] for h in range(2)]
        out = []
        for h in range(2):
            run, acc = carry[2 * h], carry[2 * h + 1]
            for c, st, (zn, suffix) in zip(chunks, stricts, parts[h]):
                run, acc = weigh(h, c, st, zn, suffix, run, acc)
            out += [run, acc]
        return tuple(out)

    def block(k):
        carry = tuple(jnp.zeros((tq, LANES), F32) for _ in range(4))
        diag = list(range(n_diag - 1, -1, -1))
        carry = step([k * n_diag + d for d in diag], [col + d * tk < row for d in diag], carry)
        for n in range(k):
            newest = k * n_diag - 1 - n * n_diag
            carry = step([newest - d for d in range(n_diag)], [None] * n_diag, carry)
        o_ref[0] = jnp.where(lane_q < HEAD_DIM, carry[1], carry[3]).astype(o_ref.dtype)

    for k in range(k_ref.shape[1] // tq):
        pl.when(i == k)(functools.partial(block, k))


def _sb_attn(proj, tq, tk):
    b, s, _ = proj.shape
    nq = s // tq
    qc, kc, vc = OFF_QA // LANES, OFF_KA // LANES, OFF_VA // LANES
    return pl.pallas_call(
        functools.partial(_sb_kernel, tq=tq, tk=tk),
        grid=(b, 2, nq),
        in_specs=[
            pl.BlockSpec((1, tq, LANES), lambda bi, p, i: (bi, i, qc + p)),
            pl.BlockSpec((1, s, LANES), lambda bi, p, i: (bi, 0, kc + p)),
            pl.BlockSpec((1, s, LANES), lambda bi, p, i: (bi, 0, vc + p)),
        ],
        out_specs=pl.BlockSpec((1, tq, LANES), lambda bi, p, i: (bi, i, p)),
        out_shape=jax.ShapeDtypeStruct((b, s, BW), BF16),
        compiler_params=_cparams(("parallel", "parallel", "arbitrary")),
        name="sb_attn",
    )(proj, proj, proj)


def _merge_kernel(x_ref, g_ref, ya_ref, yb_ref, yc_ref, yd_ref, wg_ref, wb_ref, wo_ref, o_ref):
    x = x_ref[...]
    h = _rms(x, g_ref[...]).astype(BF16)
    merged = None
    for n, y_ref in enumerate((ya_ref, yb_ref, yc_ref, yd_ref)):
        br = _dot(y_ref[...], wb_ref[n])
        gl = _dot(h, wg_ref[:, n * D_MODEL:(n + 1) * D_MODEL])
        term = br * (0.5 + 0.5 * jnp.tanh(0.5 * gl))
        merged = term if merged is None else merged + term
    o_ref[...] = x + _dot(merged.astype(BF16), wo_ref[...])


def _merge_out(x2d, g, ys, wg, wb, wo, layer, tm):
    n, d = x2d.shape
    resident = pl.Buffered(1)
    return pl.pallas_call(
        _merge_kernel,
        grid=(n // tm,),
        in_specs=[pl.BlockSpec((tm, d), lambda i: (i, 0)),
                  pl.BlockSpec((1, d), lambda i: (0, 0))]
        + [pl.BlockSpec((tm, BW), lambda i: (i, 0))] * 4
        + [
            pl.BlockSpec((None, d, GATE_W), lambda i: (layer, 0, 0), pipeline_mode=resident),
            pl.BlockSpec((None, 4, BW, d), lambda i: (layer, 0, 0, 0), pipeline_mode=resident),
            pl.BlockSpec((None, d, d), lambda i: (layer, 0, 0), pipeline_mode=resident),
        ],
        out_specs=pl.BlockSpec((tm, d), lambda i: (i, 0)),
        out_shape=jax.ShapeDtypeStruct((n, d), F32),
        compiler_params=_cparams(("parallel",)),
        name="merge_out",
    )(x2d, g, *ys, wg, wb, wo)


def _mlp_kernel(x_ref, g_ref, wu_ref, wd_ref, gf_ref, o_ref, *, final_norm, tf):
    x = x_ref[...]
    h = _rms(x, g_ref[...]).astype(BF16)
    y = x
    for c in range(wu_ref.shape[1] // tf):
        up = _dot(h, wu_ref[:, c * tf:(c + 1) * tf])
        act = jnp.square(jnp.maximum(up, 0.0)).astype(BF16)
        y = y + _dot(act, wd_ref[c * tf:(c + 1) * tf, :])
    if final_norm:
        y = _rms(y, gf_ref[...])
    o_ref[...] = y


def _mlp(x2d, g, wu, wd, layer, gf, tm, tf, final_norm):
    n, d = x2d.shape
    dff = wu.shape[2]
    resident = pl.Buffered(1)
    return pl.pallas_call(
        functools.partial(_mlp_kernel, final_norm=final_norm, tf=tf),
        grid=(n // tm,),
        in_specs=[
            pl.BlockSpec((tm, d), lambda i: (i, 0)),
            pl.BlockSpec((1, d), lambda i: (0, 0)),
            pl.BlockSpec((None, d, dff), lambda i: (layer, 0, 0), pipeline_mode=resident),
            pl.BlockSpec((None, dff, d), lambda i: (layer, 0, 0), pipeline_mode=resident),
            pl.BlockSpec((1, d), lambda i: (0, 0)),
        ],
        out_specs=pl.BlockSpec((tm, d), lambda i: (i, 0)),
        out_shape=jax.ShapeDtypeStruct((n, d), F32),
        compiler_params=_cparams(("parallel",)),
        name="mlp",
    )(x2d, g, wu, wd, gf)


def _col_scale():
    cs = np.ones((1, W_MAIN), np.float32)
    cs[0, OFF_QA:OFF_QA + BW] = HEAD_DIM ** -0.5
    cs[0, OFF_QB:OFF_QB + N_GROUPS * BW] = HEAD_DIM ** -0.5 * LOG2E
    cs[0, OFF_QC:OFF_QC + BW] = HEAD_DIM ** -0.5 * LOG2E
    cs[0, OFF_QD:OFF_QD + BW] = DIFF_QK_DIM ** -0.5 * LOG2E
    return cs


def kernel(x, mix_norm_g, w_in, b_forget, lambda_q1, lambda_k1, lambda_q2, lambda_k2,
           diff_norm_g, w_branch, w_out, mlp_norm_g, w_up, w_down, final_norm_g):
    bsz, seq, d = x.shape
    depth = w_in.shape[0]
    n = bsz * seq
    tq_big = min(512, seq)
    tm = min(1024, n)

    slopes = [float(v) for v in _alibi_slopes()]
    n0, n1 = N_HEADS, 2 * N_HEADS
    slopes_dil = (tuple(slopes[:n0]), tuple(slopes[n1:n1 + N_HEADS]), tuple(slopes[n1 + N_HEADS:]))
    slopes_diff = tuple(slopes[n0:n1])

    assert all(w // r == DB and seq % (DB * r) == 0 for w, r in DIL_PATTERNS) and n % tm == 0
    first, band = (jnp.asarray(m) for m in _dil_masks())
    dil_bias = jnp.asarray(_dil_bias_table(seq, slopes_dil))
    tril = jnp.asarray(np.tril(np.ones((2 * LANES, 2 * LANES), np.float32)), BF16)
    colscale = jnp.asarray(_col_scale())

    w_a, w_d, w_g, wf_hi, wf_lo = _split_w_in(w_in, 256)
    w_up_bf = _cast_bf16(w_up.reshape(depth * d, D_FF), 256).reshape(depth, d, D_FF)
    w_down_bf = _cast_bf16(w_down.reshape(depth * D_FF, d), 1024).reshape(depth, D_FF, d)
    w_branch_bf = _cast_bf16(w_branch.reshape(depth * 4 * BW, d), 1024).reshape(depth, 4, BW, d)
    w_out_bf = _cast_bf16(w_out.reshape(depth * d, d), 1024).reshape(depth, d, d)

    x2d = x.reshape(n, d)
    for l in range(depth):
        lambda_init = 0.8 - 0.6 * math.exp(-0.3 * l)
        bf = jnp.pad(b_forget[l].astype(F32), (0, LANES - N_HEADS)).reshape(1, LANES)
        g_mix = mix_norm_g[l].reshape(1, d)

        proj, pg1, pg2, logf = _norm_proj(x2d, g_mix, w_a, w_d, wf_hi, wf_lo, l, colscale, bf, bsz,
                                          min(512, seq))
        cum = _forget_cum(logf, tril, bsz)
        proj3 = proj.reshape(bsz, seq, W_MAIN)

        y_a = _sb_attn(proj3, tq_big, min(256, seq))
        y_b = _dil_attn(proj3, pg1.reshape(bsz, seq, -1), pg2.reshape(bsz, seq, -1), dil_bias,
                        first, band, slopes_dil)
        y_c = _fox_attn(proj3, cum, tq_big)
        lam_params = jnp.stack([lambda_q1[l], lambda_k1[l], lambda_q2[l], lambda_k2[l]]).astype(F32)
        g2 = jnp.tile(diff_norm_g[l].astype(F32), 2).reshape(1, LANES)
        y_d = _diff_attn(proj3, lam_params, g2, tq_big, slopes_diff, 1.0 - lambda_init, lambda_init)

        ys = [y.reshape(n, BW) for y in (y_a, y_b, y_c, y_d)]
        x2d = _merge_out(x2d, g_mix, ys, w_g, w_branch_bf, w_out_bf, l, min(512, n))
        x2d = _mlp(x2d, mlp_norm_g[l].reshape(1, d), w_up_bf, w_down_bf, l,
                   final_norm_g.reshape(1, d), tm, 1024,
                   final_norm=(l == depth - 1))
    return x2d.reshape(bsz, seq, d)
```

```python
import functools
import math

import ml_dtypes
import numpy as np
import jax
import jax.numpy as jnp
from jax import lax
from jax.experimental import pallas as pl
from jax.experimental.pallas import tpu as pltpu

F32 = jnp.float32
BF16 = jnp.bfloat16

D_MODEL = 1024
HEAD_DIM = 64
N_HEADS = 4
DIL_PATTERNS = ((128, 1), (512, 4), (2048, 16))
N_GROUPS = len(DIL_PATTERNS)
DIFF_QK_DIM = HEAD_DIM // 2
D_FF = 4 * D_MODEL
RMS_EPS = 1e-6
N_ALIBI = N_GROUPS * N_HEADS + N_HEADS

LANES = 128
BW = N_HEADS * HEAD_DIM
NEG = -1e30
LOG2E = math.log2(math.e)
DB = 128
VMEM_LIMIT = 56 * 1024 * 1024

OFF_QA, OFF_KA, OFF_VA = 0, BW, 2 * BW
OFF_QB, OFF_KB, OFF_VB = 3 * BW, 6 * BW, 9 * BW
OFF_QC, OFF_KC, OFF_VC = 12 * BW, 13 * BW, 14 * BW
OFF_QD, OFF_KD, OFF_VD = 15 * BW, 16 * BW, 17 * BW
W_MAIN = 18 * BW
GATE_W = 4 * D_MODEL
F_COL = 15 * BW
N_PIECES = 4


def _alibi_slopes():
    return 2.0 ** (-8.0 * np.arange(1, N_ALIBI + 1) / N_ALIBI)


def _cparams(sem):
    return pltpu.CompilerParams(dimension_semantics=sem, vmem_limit_bytes=VMEM_LIMIT)


def _rms(x, g):
    ms = jnp.mean(x * x, axis=-1, keepdims=True)
    return x * lax.rsqrt(ms + RMS_EPS) * g


def _split3(val):
    p1 = val.astype(BF16).astype(F32)
    r1 = val - p1
    p2 = r1.astype(BF16).astype(F32)
    p3 = (r1 - p2).astype(BF16).astype(F32)
    return p1, p2, p3


def _split2(val):
    hi = val.astype(BF16)
    lo = (val - hi.astype(F32)).astype(BF16)
    return hi, lo


def _dot_nt(a, b):
    return lax.dot_general(a, b, (((1,), (1,)), ((), ())), preferred_element_type=F32)


def _dot(a, b):
    return jnp.dot(a, b, preferred_element_type=F32)


def _fold_max(s):
    m = s[:, :LANES]
    for j in range(1, s.shape[1] // LANES):
        m = jnp.maximum(m, s[:, j * LANES:(j + 1) * LANES])
    return m


def _cast_kernel(x_ref, o_ref):
    o_ref[...] = x_ref[...].astype(o_ref.dtype)


def _cast_bf16(w2d, tr):
    r, c = w2d.shape
    return pl.pallas_call(
        _cast_kernel,
        grid=(r // tr,),
        in_specs=[pl.BlockSpec((tr, c), lambda i: (i, 0))],
        out_specs=pl.BlockSpec((tr, c), lambda i: (i, 0)),
        out_shape=jax.ShapeDtypeStruct((r, c), BF16),
        compiler_params=_cparams(("parallel",)),
        name="cast_bf16",
    )(w2d)


def _w_in_section_kernel(a_ref, o_ref, *, depth):
    for l in range(depth):
        for j in range(a_ref.shape[1] // depth):
            o_ref[l, j * LANES:(j + 1) * LANES, :] = a_ref[:, j * depth + l, :].T.astype(o_ref.dtype)


def _w_in_section(w_cols, depth, col0, ncols, cb):
    _, rows, _ = w_cols.shape
    d = rows // depth * LANES
    return pl.pallas_call(
        functools.partial(_w_in_section_kernel, depth=depth),
        grid=(ncols // cb,),
        in_specs=[pl.BlockSpec((pl.Element(cb), pl.Element(rows), pl.Element(LANES)),
                               lambda i: (col0 + i * cb, 0, 0))],
        out_specs=pl.BlockSpec((depth, d, cb), lambda i: (0, 0, i)),
        out_shape=jax.ShapeDtypeStruct((depth, d, ncols), BF16),
        compiler_params=_cparams(("parallel",)),
        name="w_in_section",
    )(w_cols)


def _norm_proj_kernel(x_ref, g_ref, wa_ref, wd_ref, cs_ref, fhi_ref, flo_ref, fb_ref,
                      o_ref, g1_ref, g2_ref, logf_ref, stage_ref, *, tn):
    tm = x_ref.shape[0]
    h32 = _rms(x_ref[...], g_ref[...])
    h, h_lo = _split2(h32)
    y = (_dot(h, fhi_ref[...]) + _dot(h, flo_ref[...]) + _dot(h_lo, fhi_ref[...])) + fb_ref[...]
    logf_ref[...] = jnp.minimum(y, 0.0) - jnp.log(1.0 + jnp.exp(-jnp.abs(y)))
    na = wa_ref.shape[1] // tn
    for c in range(na + wd_ref.shape[1] // tn):
        w = wa_ref[:, c * tn:(c + 1) * tn] if c < na else wd_ref[:, (c - na) * tn:(c - na + 1) * tn]
        cols = slice(c * tn, (c + 1) * tn)
        res = _dot(h, w) * cs_ref[:, cols]
        o_ref[:, cols] = res.astype(o_ref.dtype)
        kind = c - OFF_QB // tn
        if 0 <= kind < 3:
            per = BW // LANES
            for j in range((N_GROUPS - 1) * per):
                stage_ref[j] = res[:, BW + j * LANES:BW + (j + 1) * LANES]
            for gi, out_ref in enumerate((g1_ref, g2_ref)):
                dil = DIL_PATTERNS[gi + 1][1]
                for cl in range(dil):
                    for j in range(per):
                        rows = stage_ref[gi * per + j, pl.ds(cl, tm // dil, stride=dil), :]
                        lanes = slice(kind * BW + j * LANES, kind * BW + (j + 1) * LANES)
                        out_ref[0, cl, :, lanes] = rows.astype(out_ref.dtype)


def _norm_proj(x2d, g, w_a, w_d, wf_hi, wf_lo, layer, colscale, bf, bsz, tm):
    n, d = x2d.shape
    s = n // bsz
    per_batch = s // tm
    tn = N_GROUPS * BW
    resident = pl.Buffered(1)
    d1, d2 = DIL_PATTERNS[1][1], DIL_PATTERNS[2][1]
    return pl.pallas_call(
        functools.partial(_norm_proj_kernel, tn=tn),
        grid=(n // tm,),
        in_specs=[
            pl.BlockSpec((tm, d), lambda i: (i, 0)),
            pl.BlockSpec((1, d), lambda i: (0, 0)),
            pl.BlockSpec((None,) + w_a.shape[1:], lambda i: (layer, 0, 0), pipeline_mode=resident),
            pl.BlockSpec((None,) + w_d.shape[1:], lambda i: (layer, 0, 0), pipeline_mode=resident),
            pl.BlockSpec((1, W_MAIN), lambda i: (0, 0)),
            pl.BlockSpec((None, d, LANES), lambda i: (layer, 0, 0)),
            pl.BlockSpec((None, d, LANES), lambda i: (layer, 0, 0)),
            pl.BlockSpec((1, LANES), lambda i: (0, 0)),
        ],
        out_specs=[
            pl.BlockSpec((tm, W_MAIN), lambda i: (i, 0)),
            pl.BlockSpec((1, d1, tm // d1, tn), lambda i: (i // per_batch, 0, i % per_batch, 0)),
            pl.BlockSpec((1, d2, tm // d2, tn), lambda i: (i // per_batch, 0, i % per_batch, 0)),
            pl.BlockSpec((tm, LANES), lambda i: (i, 0)),
        ],
        out_shape=[
            jax.ShapeDtypeStruct((n, W_MAIN), BF16),
            jax.ShapeDtypeStruct((bsz, d1, s // d1, tn), BF16),
            jax.ShapeDtypeStruct((bsz, d2, s // d2, tn), BF16),
            jax.ShapeDtypeStruct((n, LANES), F32),
        ],
        scratch_shapes=[pltpu.VMEM(((N_GROUPS - 1) * BW // LANES, tm, LANES), F32)],
        compiler_params=_cparams(("parallel",)),
        name="norm_proj",
    )(x2d, g, w_a, w_d, colscale, wf_hi, wf_lo, bf)


def _cumsum_kernel(l_ref, tril_ref, o_ref):
    tril = tril_ref[...]
    blk = tril.shape[0]
    carry = jnp.zeros((1, LANES), F32)
    for j in range(l_ref.shape[1] // blk):
        l1, l2, l3 = _split3(l_ref[0, j * blk:(j + 1) * blk, :])
        c = (_dot(tril, l1.astype(BF16)) + _dot(tril, l2.astype(BF16))
             + _dot(tril, l3.astype(BF16))) + carry
        o_ref[0, j * blk:(j + 1) * blk, :] = c
        carry = c[blk - 1:blk, :]


def _forget_cum(logf, tril, bsz):
    s = logf.shape[0] // bsz
    return pl.pallas_call(
        _cumsum_kernel,
        grid=(bsz,),
        in_specs=[
            pl.BlockSpec((1, s, LANES), lambda i: (i, 0, 0)),
            pl.BlockSpec(tril.shape, lambda i: (0, 0)),
        ],
        out_specs=pl.BlockSpec((1, s, LANES), lambda i: (i, 0, 0)),
        out_shape=jax.ShapeDtypeStruct((bsz, s, LANES), F32),
        compiler_params=_cparams(("parallel",)),
        name="forget_cumsum",
    )(logf.reshape(bsz, s, LANES), tril)


def _aug(base, keep, lane, lane0, pieces):
    out = jnp.where(keep, base, 0.0)
    for i, p in enumerate(pieces):
        out = jnp.where(lane == lane0 + i, p, out)
    return out


def _ones_at(lane, lane0, n):
    return jnp.where((lane >= lane0) & (lane < lane0 + n), 1.0, 0.0)


def _v_aug(v, lane, h):
    own = (lane < HEAD_DIM) if h == 0 else (lane >= HEAD_DIM)
    return jnp.where(own, v, 1.0).astype(BF16)


def _normalize_pair(acc0, acc1, lane):
    den0 = pltpu.roll(acc0, HEAD_DIM, 1)
    den1 = pltpu.roll(acc1, HEAD_DIM, 1)
    return jnp.where(lane < HEAD_DIM, acc0 / den0, acc1 / den1)


def _two_pass_attention(qa, ka_ref, va_ref, s_ref, v_of_map, i, tq, finish):
    n_maps = len(qa)
    tk = tq
    nq = ka_ref.shape[1] // tk
    row = lax.broadcasted_iota(jnp.int32, (tq, tk), 0)
    col = lax.broadcasted_iota(jnp.int32, (tq, tk), 1)
    causal = col <= row

    def block(k):
        mcol = [None] * n_maps
        for c in range(k + 1):
            for m in range(n_maps):
                s = _dot_nt(qa[m], ka_ref[m, c * tk:(c + 1) * tk, :])
                if c == k:
                    s = jnp.where(causal, s, NEG)
                s_ref[m, c] = s
                fm = _fold_max(s)
                mcol[m] = fm if mcol[m] is None else jnp.maximum(mcol[m], fm)
        accs = []
        for m in range(n_maps):
            mrow = jnp.broadcast_to(jnp.max(mcol[m], axis=1, keepdims=True), (tq, LANES))
            mb = jnp.concatenate([mrow] * (tk // LANES), axis=1)
            acc = None
            for c in range(k + 1):
                pr = jnp.exp2(s_ref[m, c] - mb).astype(BF16)
                part = _dot(pr, va_ref[v_of_map[m], c * tk:(c + 1) * tk, :])
                acc = part if acc is None else acc + part
            accs.append(acc)
        finish(accs)

    for k in range(nq):
        pl.when(i == k)(functools.partial(block, k))


def _fox_kernel(q_ref, k_ref, v_ref, cum_ref, o_ref, ka_ref, va_ref, s_ref, *, tq):
    p = pl.program_id(1)
    i = pl.program_id(2)
    s_len = k_ref.shape[1]

    @pl.when(i == 0)
    def _prep():
        lane = lax.broadcasted_iota(jnp.int32, (s_len, LANES), 1)
        k = k_ref[0].astype(F32)
        v = v_ref[0].astype(F32)
        cum = cum_ref[0]
        for h in range(2):
            own = (lane < HEAD_DIM) if h == 0 else (lane >= HEAD_DIM)
            spare = HEAD_DIM if h == 0 else 0
            c = jnp.sum(jnp.where(lane == 2 * p + h, cum, 0.0), axis=1, keepdims=True)
            pieces = _split3(jnp.broadcast_to(c * (-LOG2E), (s_len, LANES)))
            ka_ref[h] = _aug(k, own, lane, spare, pieces).astype(BF16)
            va_ref[h] = _v_aug(v, lane, h)

    lane_q = lax.broadcasted_iota(jnp.int32, (tq, LANES), 1)
    q = q_ref[0].astype(F32)
    qa = []
    for h in range(2):
        own = (lane_q < HEAD_DIM) if h == 0 else (lane_q >= HEAD_DIM)
        spare = HEAD_DIM if h == 0 else 0
        qa.append(jnp.where(own, q, _ones_at(lane_q, spare, 3)).astype(BF16))

    def finish(accs):
        o_ref[0] = _normalize_pair(accs[0], accs[1], lane_q).astype(o_ref.dtype)

    _two_pass_attention(qa, ka_ref, va_ref, s_ref, (0, 1), i, tq, finish)


def _fox_attn(proj, cum, tq):
    b, s, _ = proj.shape
    nq = s // tq
    qc, kc, vc = OFF_QC // LANES, OFF_KC // LANES, OFF_VC // LANES
    return pl.pallas_call(
        functools.partial(_fox_kernel, tq=tq),
        grid=(b, 2, nq),
        in_specs=[
            pl.BlockSpec((1, tq, LANES), lambda bi, p, i: (bi, i, qc + p)),
            pl.BlockSpec((1, s, LANES), lambda bi, p, i: (bi, 0, kc + p)),
            pl.BlockSpec((1, s, LANES), lambda bi, p, i: (bi, 0, vc + p)),
            pl.BlockSpec((1, s, LANES), lambda bi, p, i: (bi, 0, 0)),
        ],
        out_specs=pl.BlockSpec((1, tq, LANES), lambda bi, p, i: (bi, i, p)),
        out_shape=jax.ShapeDtypeStruct((b, s, BW), BF16),
        scratch_shapes=[
            pltpu.VMEM((2, s, LANES), BF16),
            pltpu.VMEM((2, s, LANES), BF16),
            pltpu.VMEM((2, nq, tq, tq), F32),
        ],
        compiler_params=_cparams(("parallel", "parallel", "arbitrary")),
        name="fox_attn",
    )(proj, proj, proj, cum)


def _diff_kernel(q_ref, k_ref, v_ref, lam_ref, g_ref, o_ref, ka_ref, va_ref, s_ref, *,
                 tq, slopes, out_scale, lambda_init):
    p = pl.program_id(1)
    i = pl.program_id(2)
    s_len = k_ref.shape[1]
    dq = DIFF_QK_DIM

    @pl.when(i == 0)
    def _prep():
        lane = lax.broadcasted_iota(jnp.int32, (s_len, LANES), 1)
        pos = lax.broadcasted_iota(jnp.int32, (s_len, LANES), 0).astype(F32)
        k = k_ref[0].astype(F32)
        v = v_ref[0].astype(F32)
        for h in range(2):
            slope = jnp.where(p == 0, slopes[h], slopes[2 + h]).astype(F32)
            pieces = _split3(pos * (slope * LOG2E))
            for c in range(2):
                m = 2 * h + c
                own = (lane >= m * dq) & (lane < (m + 1) * dq)
                spare = ((m + 1) % 4) * dq
                ka_ref[m] = _aug(k, own, lane, spare, pieces).astype(BF16)
            va_ref[h] = _v_aug(v, lane, h)

    lane_q = lax.broadcasted_iota(jnp.int32, (tq, LANES), 1)
    q = q_ref[0].astype(F32)
    qa = []
    for m in range(4):
        own = (lane_q >= m * dq) & (lane_q < (m + 1) * dq)
        spare = ((m + 1) % 4) * dq
        qa.append(jnp.where(own, q, _ones_at(lane_q, spare, 3)).astype(BF16))

    lp = lam_ref[...]
    lam = (jnp.exp(jnp.sum(lp[0:1] * lp[1:2], axis=1, keepdims=True))
           - jnp.exp(jnp.sum(lp[2:3] * lp[3:4], axis=1, keepdims=True)) + lambda_init)
    r = lax.broadcasted_iota(jnp.int32, (LANES, LANES), 0) // HEAD_DIM
    cc = lax.broadcasted_iota(jnp.int32, (LANES, LANES), 1) // HEAD_DIM
    same = jnp.where(r == cc, 1.0, 0.0).astype(BF16)

    def finish(accs):
        o1 = _normalize_pair(accs[0], accs[2], lane_q)
        o2 = _normalize_pair(accs[1], accs[3], lane_q)
        o = o1 - lam * o2
        o2_hi, o2_lo = _split2(o * o)
        ms = (_dot(o2_hi, same) + _dot(o2_lo, same)) * (1.0 / HEAD_DIM)
        y = o * lax.rsqrt(ms + RMS_EPS) * g_ref[...]
        o_ref[0] = (y * out_scale).astype(o_ref.dtype)

    _two_pass_attention(qa, ka_ref, va_ref, s_ref, (0, 0, 1, 1), i, tq, finish)


def _diff_attn(proj, lam_params, g2, tq, slopes, out_scale, lambda_init):
    b, s, _ = proj.shape
    nq = s // tq
    qc, kc, vc = OFF_QD // LANES, OFF_KD // LANES, OFF_VD // LANES
    return pl.pallas_call(
        functools.partial(_diff_kernel, tq=tq, slopes=slopes, out_scale=out_scale,
                          lambda_init=lambda_init),
        grid=(b, 2, nq),
        in_specs=[
            pl.BlockSpec((1, tq, LANES), lambda bi, p, i: (bi, i, qc + p)),
            pl.BlockSpec((1, s, LANES), lambda bi, p, i: (bi, 0, kc + p)),
            pl.BlockSpec((1, s, LANES), lambda bi, p, i: (bi, 0, vc + p)),
            pl.BlockSpec((4, DIFF_QK_DIM), lambda bi, p, i: (0, 0)),
            pl.BlockSpec((1, LANES), lambda bi, p, i: (0, 0)),
        ],
        out_specs=pl.BlockSpec((1, tq, LANES), lambda bi, p, i: (bi, i, p)),
        out_shape=jax.ShapeDtypeStruct((b, s, BW), BF16),
        scratch_shapes=[
            pltpu.VMEM((4, s, LANES), BF16),
            pltpu.VMEM((2, s, LANES), BF16),
            pltpu.VMEM((4, nq, tq, tq), F32),
        ],
        compiler_params=_cparams(("parallel", "parallel", "arbitrary")),
        name="diff_attn",
    )(proj, proj, proj, lam_params, g2)


def _dil_masks():
    t = np.arange(DB)[:, None]
    s = np.arange(DB)[None, :]
    first = np.where(s <= t, 0.0, NEG)
    s2 = np.arange(2 * DB)[None, :]
    dist = t + DB - s2
    band = np.where((dist >= 0) & (dist <= DB), 0.0, NEG)
    return first.astype(np.float32), band.astype(np.float32)


def _dil_bias_table(seq, slopes):
    out = np.zeros((N_GROUPS, N_HEADS, seq, LANES), np.float32)
    rho = np.arange(seq)
    for g, (window, dil) in enumerate(DIL_PATTERNS):
        cls_len = seq // dil
        pos = (rho % cls_len) * dil + rho // cls_len
        for hh in range(N_HEADS):
            spare = HEAD_DIM if hh % 2 == 0 else 0
            rest = pos.astype(np.float64) * (slopes[g][hh] * LOG2E)
            for i in range(N_PIECES):
                piece = rest.astype(ml_dtypes.bfloat16).astype(np.float64)
                out[g, hh, :, spare + i] = piece
                rest = rest - piece
    return out.astype(ml_dtypes.bfloat16)


def _dil_kernel(*refs, slopes):
    q_refs, k_refs, v_refs = refs[0:3], refs[3:6], refs[6:9]
    (bias_ref, first_ref, band_ref, o_ref, qa_ref, ka_ref, va_ref, og_ref, lse_ref,
     s_ref, m_ref, acc_ref) = refs[9:]
    p = pl.program_id(1)
    s_len = k_refs[0].shape[1]
    n_blk = s_len // DB

    def slope_of(g, h):
        return jnp.where(p == 0, slopes[g][h], slopes[g][2 + h]).astype(F32)

    lane1 = lax.broadcasted_iota(jnp.int32, (1, LANES), 1)
    for h in range(2):
        own_l = (lane1 < HEAD_DIM) if h == 0 else (lane1 >= HEAD_DIM)
        spare = HEAD_DIM if h == 0 else 0
        own = jnp.where(own_l, 1.0, 0.0).astype(BF16)
        other = jnp.where(own_l, 0.0, 1.0).astype(BF16)
        ones = _ones_at(lane1, spare, N_PIECES).astype(BF16)
        for g in range(N_GROUPS):
            qa_ref[2 * g + h] = q_refs[g][0] * own + ones
            ka_ref[2 * g + h] = k_refs[g][0] * own + bias_ref[g, h]
            va_ref[2 * g + h] = v_refs[g][0] * own + other

    low = lax.broadcasted_iota(jnp.int32, (s_len, LANES), 1) < HEAD_DIM
    for g, (window, dil) in enumerate(DIL_PATTERNS):
        cls_blocks = n_blk // dil
        cls_len = s_len // dil

        def window_of(j):
            first = j % cls_blocks == 0
            return (j * DB if first else (j - 1) * DB), (DB if first else 2 * DB), first

        for j in range(n_blk):
            k0, kw, first = window_of(j)
            mask = first_ref[...] if first else band_ref[...]
            for h in range(2):
                gh = 2 * g + h
                s = _dot_nt(qa_ref[gh, j * DB:(j + 1) * DB, :], ka_ref[gh, k0:k0 + kw, :]) + mask
                s_ref[h, j, :, :kw] = s
                m = jnp.max(s, axis=1, keepdims=True)
                m_ref[h, j * DB:(j + 1) * DB, :] = jnp.broadcast_to(m, (DB, LANES))
        for j in range(n_blk):
            k0, kw, first = window_of(j)
            for h in range(2):
                gh = 2 * g + h
                m = m_ref[h, j * DB:(j + 1) * DB, :]
                mb = m if first else jnp.concatenate([m, m], axis=1)
                pr = jnp.exp2(s_ref[h, j, :, :kw] - mb).astype(BF16)
                acc_ref[h, j * DB:(j + 1) * DB, :] = _dot(pr, va_ref[gh, k0:k0 + kw, :])
        acc0, acc1 = acc_ref[0], acc_ref[1]
        num = jnp.where(low, acc0, acc1)
        den = jnp.where(low, pltpu.roll(acc0, HEAD_DIM, 1), pltpu.roll(acc1, HEAD_DIM, 1))
        out_g = num / den
        lse_g = jnp.where(low, m_ref[0], m_ref[1]) + jnp.log(den) * LOG2E
        for c in range(dil):
            rows = pl.ds(c, cls_len, stride=dil) if dil > 1 else pl.ds(0, cls_len)
            og_ref[g, rows, :] = out_g[c * cls_len:(c + 1) * cls_len]
            lse_ref[g, rows, :] = lse_g[c * cls_len:(c + 1) * cls_len]

    lane_s = lax.broadcasted_iota(jnp.int32, (s_len, LANES), 1)
    t = lax.broadcasted_iota(jnp.int32, (s_len, LANES), 0).astype(F32)
    lses = []
    for g in range(N_GROUPS):
        slope_lane = jnp.where(lane_s < HEAD_DIM, slope_of(g, 0), slope_of(g, 1)) * LOG2E
        lses.append(lse_ref[g] - slope_lane * t)
    top = jnp.maximum(jnp.maximum(lses[0], lses[1]), lses[2])
    wsum = jnp.zeros((s_len, LANES), F32)
    out = jnp.zeros((s_len, LANES), F32)
    for g in range(N_GROUPS):
        w = jnp.exp2(lses[g] - top)
        wsum = wsum + w
        out = out + w * og_ref[g]
    o_ref[0] = (out / wsum).astype(o_ref.dtype)


def _dil_attn(proj, g1, g2, bias, first, band, slopes):
    b, s, _ = proj.shape
    per = BW // LANES

    def spec(blk):
        return pl.BlockSpec((1, s, LANES), lambda bi, p: (bi, 0, blk + p))

    arrays, in_specs = [], []
    for kind, off in enumerate((OFF_QB, OFF_KB, OFF_VB)):
        arrays += [proj, g1, g2]
        in_specs += [spec(off // LANES), spec(kind * per), spec(kind * per)]
    in_specs += [pl.BlockSpec((N_GROUPS, 2, s, LANES), lambda bi, p: (0, p, 0, 0)),
                 pl.BlockSpec(first.shape, lambda bi, p: (0, 0)),
                 pl.BlockSpec(band.shape, lambda bi, p: (0, 0))]
    return pl.pallas_call(
        functools.partial(_dil_kernel, slopes=slopes),
        grid=(b, 2),
        in_specs=in_specs,
        out_specs=pl.BlockSpec((1, s, LANES), lambda bi, p: (bi, 0, p)),
        out_shape=jax.ShapeDtypeStruct((b, s, BW), BF16),
        scratch_shapes=[
            pltpu.VMEM((2 * N_GROUPS, s, LANES), BF16),
            pltpu.VMEM((2 * N_GROUPS, s, LANES), BF16),
            pltpu.VMEM((2 * N_GROUPS, s, LANES), BF16),
            pltpu.VMEM((N_GROUPS, s, LANES), F32),
            pltpu.VMEM((N_GROUPS, s, LANES), F32),
            pltpu.VMEM((2, s // DB, DB, 2 * DB), F32),
            pltpu.VMEM((2, s, LANES), F32),
            pltpu.VMEM((2, s, LANES), F32),
        ],
        compiler_params=_cparams(("parallel", "parallel")),
        name="dil_attn",
    )(*arrays, bias, first, band)


def _sb_kernel(q_ref, k_ref, v_ref, o_ref, *, tq, tk):
    i = pl.program_id(2)
    n_diag = tq // tk
    lane_q = lax.broadcasted_iota(jnp.int32, (tq, LANES), 1)
    q = q_ref[0].astype(F32)
    row = lax.broadcasted_iota(jnp.int32, (tq, tk), 0)
    col = lax.broadcasted_iota(jnp.int32, (tq, tk), 1)
    tr = lax.broadcasted_iota(jnp.int32, (tk, tk), 0)
    tc = lax.broadcasted_iota(jnp.int32, (tk, tk), 1)
    tri = jnp.where(tr >= tc, 1.0, 0.0).astype(BF16)
    qn = [jnp.where((lane_q < HEAD_DIM) if h == 0 else (lane_q >= HEAD_DIM), -q, 0.0).astype(BF16)
          for h in range(2)]

    def scores(h, c, strict):
        kc = k_ref[0, c * tk:(c + 1) * tk, :]
        zn = _dot_nt(qn[h], kc)
        lk = jnp.minimum(zn, 0.0) - jnp.log(1.0 + jnp.exp(-jnp.abs(zn)))
        if strict is not None:
            lk = jnp.where(strict, lk, 0.0)
        return zn, _dot(lk.astype(BF16), tri)

    def weigh(h, c, strict, zn, suffix, run, acc):
        vc = v_ref[0, c * tk:(c + 1) * tk, :]
        run_b = jnp.concatenate([run] * (tk // LANES), axis=1)
        a = jnp.exp(suffix + run_b - zn)
        if strict is not None:
            a = jnp.where(strict, a, 0.0)
        acc = acc + _dot(a.astype(BF16), vc)
        return run + jnp.broadcast_to(suffix[:, 0:1], (tq, LANES)), acc

    def step(chunks, stricts, carry):
        parts = [[scores(h, c, st) for c, st in zip(chunks, stricts)] for h in range(2)]
        out = []
        for h in range(2):
            run, acc = carry[2 * h], carry[2 * h + 1]
            for c, st, (zn, suffix) in zip(chunks, stricts, parts[h]):
                run, acc = weigh(h, c, st, zn, suffix, run, acc)
            out += [run, acc]
        return tuple(out)

    def block(k):
        carry = tuple(jnp.zeros((tq, LANES), F32) for _ in range(4))
        diag = list(range(n_diag - 1, -1, -1))
        carry = step([k * n_diag + d for d in diag], [col + d * tk < row for d in diag], carry)
        for n in range(k):
            newest = k * n_diag - 1 - n * n_diag
            carry = step([newest - d for d in range(n_diag)], [None] * n_diag, carry)
        o_ref[0] = jnp.where(lane_q < HEAD_DIM, carry[1], carry[3]).astype(o_ref.dtype)

    for k in range(k_ref.shape[1] // tq):
        pl.when(i == k)(functools.partial(block, k))


def _sb_attn(proj, tq, tk):
    b, s, _ = proj.shape
    nq = s // tq
    qc, kc, vc = OFF_QA // LANES, OFF_KA // LANES, OFF_VA // LANES
    return pl.pallas_call(
        functools.partial(_sb_kernel, tq=tq, tk=tk),
        grid=(b, 2, nq),
        in_specs=[
            pl.BlockSpec((1, tq, LANES), lambda bi, p, i: (bi, i, qc + p)),
            pl.BlockSpec((1, s, LANES), lambda bi, p, i: (bi, 0, kc + p)),
            pl.BlockSpec((1, s, LANES), lambda bi, p, i: (bi, 0, vc + p)),
        ],
        out_specs=pl.BlockSpec((1, tq, LANES), lambda bi, p, i: (bi, i, p)),
        out_shape=jax.ShapeDtypeStruct((b, s, BW), BF16),
        compiler_params=_cparams(("parallel", "parallel", "arbitrary")),
        name="sb_attn",
    )(proj, proj, proj)


def _merge_kernel(x_ref, g_ref, ya_ref, yb_ref, yc_ref, yd_ref, wg_ref, wb_ref, wo_ref, o_ref):
    x = x_ref[...]
    h = _rms(x, g_ref[...]).astype(BF16)
    merged = None
    for n, y_ref in enumerate((ya_ref, yb_ref, yc_ref, yd_ref)):
        br = _dot(y_ref[...], wb_ref[n])
        gl = _dot(h, wg_ref[:, n * D_MODEL:(n + 1) * D_MODEL])
        term = br * (0.5 + 0.5 * jnp.tanh(0.5 * gl))
        merged = term if merged is None else merged + term
    o_ref[...] = x + _dot(merged.astype(BF16), wo_ref[...])


def _merge_out(x2d, g, ys, wg, wb, wo, layer, tm):
    n, d = x2d.shape
    resident = pl.Buffered(1)
    return pl.pallas_call(
        _merge_kernel,
        grid=(n // tm,),
        in_specs=[pl.BlockSpec((tm, d), lambda i: (i, 0)),
                  pl.BlockSpec((1, d), lambda i: (0, 0))]
        + [pl.BlockSpec((tm, BW), lambda i: (i, 0))] * 4
        + [
            pl.BlockSpec((None, d, GATE_W), lambda i: (layer, 0, 0), pipeline_mode=resident),
            pl.BlockSpec((None, 4, BW, d), lambda i: (layer, 0, 0, 0), pipeline_mode=resident),
            pl.BlockSpec((None, d, d), lambda i: (layer, 0, 0), pipeline_mode=resident),
        ],
        out_specs=pl.BlockSpec((tm, d), lambda i: (i, 0)),
        out_shape=jax.ShapeDtypeStruct((n, d), F32),
        compiler_params=_cparams(("parallel",)),
        name="merge_out",
    )(x2d, g, *ys, wg, wb, wo)


def _mlp_kernel(x_ref, g_ref, wu_ref, wd_ref, gf_ref, o_ref, *, final_norm, tf):
    x = x_ref[...]
    h = _rms(x, g_ref[...]).astype(BF16)
    y = x
    for c in range(wu_ref.shape[1] // tf):
        up = _dot(h, wu_ref[:, c * tf:(c + 1) * tf])
        act = jnp.square(jnp.maximum(up, 0.0)).astype(BF16)
        y = y + _dot(act, wd_ref[c * tf:(c + 1) * tf, :])
    if final_norm:
        y = _rms(y, gf_ref[...])
    o_ref[...] = y


def _mlp(x2d, g, wu, wd, layer, gf, tm, tf, final_norm):
    n, d = x2d.shape
    dff = wu.shape[2]
    resident = pl.Buffered(1)
    return pl.pallas_call(
        functools.partial(_mlp_kernel, final_norm=final_norm, tf=tf),
        grid=(n // tm,),
        in_specs=[
            pl.BlockSpec((tm, d), lambda i: (i, 0)),
            pl.BlockSpec((1, d), lambda i: (0, 0)),
            pl.BlockSpec((None, d, dff), lambda i: (layer, 0, 0), pipeline_mode=resident),
            pl.BlockSpec((None, dff, d), lambda i: (layer, 0, 0), pipeline_mode=resident),
            pl.BlockSpec((1, d), lambda i: (0, 0)),
        ],
        out_specs=pl.BlockSpec((tm, d), lambda i: (i, 0)),
        out_shape=jax.ShapeDtypeStruct((n, d), F32),
        compiler_params=_cparams(("parallel",)),
        name="mlp",
    )(x2d, g, wu, wd, gf)


def _col_scale():
    cs = np.ones((1, W_MAIN), np.float32)
    cs[0, OFF_QA:OFF_QA + BW] = HEAD_DIM ** -0.5
    cs[0, OFF_QB:OFF_QB + N_GROUPS * BW] = HEAD_DIM ** -0.5 * LOG2E
    cs[0, OFF_QC:OFF_QC + BW] = HEAD_DIM ** -0.5 * LOG2E
    cs[0, OFF_QD:OFF_QD + BW] = DIFF_QK_DIM ** -0.5 * LOG2E
    return cs


def kernel(x, mix_norm_g, w_in, b_forget, lambda_q1, lambda_k1, lambda_q2, lambda_k2,
           diff_norm_g, w_branch, w_out, mlp_norm_g, w_up, w_down, final_norm_g):
    bsz, seq, d = x.shape
    depth = w_in.shape[0]
    n = bsz * seq
    tq_big = min(512, seq)
    tm = min(1024, n)

    slopes = [float(v) for v in _alibi_slopes()]
    n0, n1 = N_HEADS, 2 * N_HEADS
    slopes_dil = (tuple(slopes[:n0]), tuple(slopes[n1:n1 + N_HEADS]), tuple(slopes[n1 + N_HEADS:]))
    slopes_diff = tuple(slopes[n0:n1])

    assert all(w // r == DB and seq % (DB * r) == 0 for w, r in DIL_PATTERNS) and n % tm == 0
    first, band = (jnp.asarray(m) for m in _dil_masks())
    dil_bias = jnp.asarray(_dil_bias_table(seq, slopes_dil))
    tril = jnp.asarray(np.tril(np.ones((2 * LANES, 2 * LANES), np.float32)), BF16)
    colscale = jnp.asarray(_col_scale())

    d_end = F_COL + N_HEADS + 3 * BW
    w_cols = (w_in.reshape(depth, d // LANES, LANES, w_in.shape[2]).transpose(3, 1, 0, 2)
              .reshape(w_in.shape[2], d // LANES * depth, LANES))
    w_a = _w_in_section(w_cols, depth, 0, F_COL, 256)
    w_d = _w_in_section(w_cols, depth, F_COL + N_HEADS, 3 * BW, 256)
    w_g = _w_in_section(w_cols, depth, d_end, GATE_W, 256)
    wf = jnp.pad(w_in[:, :, F_COL:F_COL + N_HEADS], ((0, 0), (0, 0), (0, LANES - N_HEADS)))
    wf_hi = wf.astype(BF16)
    wf_lo = (wf - wf_hi.astype(F32)).astype(BF16)
    w_up_bf = _cast_bf16(w_up.reshape(depth * d, D_FF), 256).reshape(depth, d, D_FF)
    w_down_bf = _cast_bf16(w_down.reshape(depth * D_FF, d), 1024).reshape(depth, D_FF, d)
    w_branch_bf = _cast_bf16(w_branch.reshape(depth * 4 * BW, d), 1024).reshape(depth, 4, BW, d)
    w_out_bf = _cast_bf16(w_out.reshape(depth * d, d), 1024).reshape(depth, d, d)

    x2d = x.reshape(n, d)
    for l in range(depth):
        lambda_init = 0.8 - 0.6 * math.exp(-0.3 * l)
        bf = jnp.pad(b_forget[l].astype(F32), (0, LANES - N_HEADS)).reshape(1, LANES)
        g_mix = mix_norm_g[l].reshape(1, d)

        proj, pg1, pg2, logf = _norm_proj(x2d, g_mix, w_a, w_d, wf_hi, wf_lo, l, colscale, bf, bsz,
                                          min(1024, seq))
        cum = _forget_cum(logf, tril, bsz)
        proj3 = proj.reshape(bsz, seq, W_MAIN)

        y_a = _sb_attn(proj3, tq_big, min(256, seq))
        y_b = _dil_attn(proj3, pg1.reshape(bsz, seq, -1), pg2.reshape(bsz, seq, -1), dil_bias,
                        first, band, slopes_dil)
        y_c = _fox_attn(proj3, cum, tq_big)
        lam_params = jnp.stack([lambda_q1[l], lambda_k1[l], lambda_q2[l], lambda_k2[l]]).astype(F32)
        g2 = jnp.tile(diff_norm_g[l].astype(F32), 2).reshape(1, LANES)
        y_d = _diff_attn(proj3, lam_params, g2, tq_big, slopes_diff, 1.0 - lambda_init, lambda_init)

        ys = [y.reshape(n, BW) for y in (y_a, y_b, y_c, y_d)]
        x2d = _merge_out(x2d, g_mix, ys, w_g, w_branch_bf, w_out_bf, l, min(512, n))
        x2d = _mlp(x2d, mlp_norm_g[l].reshape(1, d), w_up_bf, w_down_bf, l,
                   final_norm_g.reshape(1, d), tm, 1024,
                   final_norm=(l == depth - 1))
    return x2d.reshape(bsz, seq, d)
```

```python
import functools
import math

import ml_dtypes
import numpy as np
import jax
import jax.numpy as jnp
from jax import lax
from jax.experimental import pallas as pl
from jax.experimental.pallas import tpu as pltpu

F32 = jnp.float32
BF16 = jnp.bfloat16

D_MODEL = 1024
HEAD_DIM = 64
N_HEADS = 4
DIL_PATTERNS = ((128, 1), (512, 4), (2048, 16))
N_GROUPS = len(DIL_PATTERNS)
DIFF_QK_DIM = HEAD_DIM // 2
D_FF = 4 * D_MODEL
RMS_EPS = 1e-6
N_ALIBI = N_GROUPS * N_HEADS + N_HEADS

LANES = 128
BW = N_HEADS * HEAD_DIM
NEG = -1e30
LOG2E = math.log2(math.e)
DB = 128
VMEM_LIMIT = 56 * 1024 * 1024

OFF_QA, OFF_KA, OFF_VA = 0, BW, 2 * BW
OFF_QB, OFF_KB, OFF_VB = 3 * BW, 6 * BW, 9 * BW
OFF_QC, OFF_KC, OFF_VC = 12 * BW, 13 * BW, 14 * BW
OFF_QD, OFF_KD, OFF_VD = 15 * BW, 16 * BW, 17 * BW
W_MAIN = 18 * BW
GATE_W = 4 * D_MODEL
F_COL = 15 * BW
N_PIECES = 4


def _alibi_slopes():
    return 2.0 ** (-8.0 * np.arange(1, N_ALIBI + 1) / N_ALIBI)


def _cparams(sem):
    return pltpu.CompilerParams(dimension_semantics=sem, vmem_limit_bytes=VMEM_LIMIT)


def _rms(x, g):
    ms = jnp.mean(x * x, axis=-1, keepdims=True)
    return x * lax.rsqrt(ms + RMS_EPS) * g


def _split3(val):
    p1 = val.astype(BF16).astype(F32)
    r1 = val - p1
    p2 = r1.astype(BF16).astype(F32)
    p3 = (r1 - p2).astype(BF16).astype(F32)
    return p1, p2, p3


def _split2(val):
    hi = val.astype(BF16)
    lo = (val - hi.astype(F32)).astype(BF16)
    return hi, lo


def _dot_nt(a, b):
    return lax.dot_general(a, b, (((1,), (1,)), ((), ())), preferred_element_type=F32)


def _dot(a, b):
    return jnp.dot(a, b, preferred_element_type=F32)


def _fold_max(s):
    m = s[:, :LANES]
    for j in range(1, s.shape[1] // LANES):
        m = jnp.maximum(m, s[:, j * LANES:(j + 1) * LANES])
    return m


def _cast_kernel(x_ref, o_ref):
    o_ref[...] = x_ref[...].astype(o_ref.dtype)


def _cast_bf16(w2d, tr):
    r, c = w2d.shape
    return pl.pallas_call(
        _cast_kernel,
        grid=(r // tr,),
        in_specs=[pl.BlockSpec((tr, c), lambda i: (i, 0))],
        out_specs=pl.BlockSpec((tr, c), lambda i: (i, 0)),
        out_shape=jax.ShapeDtypeStruct((r, c), BF16),
        compiler_params=_cparams(("parallel",)),
        name="cast_bf16",
    )(w2d)


def _w_in_section_kernel(a_ref, o_ref, *, depth):
    for l in range(depth):
        for j in range(a_ref.shape[1] // depth):
            o_ref[l, j * LANES:(j + 1) * LANES, :] = a_ref[:, j * depth + l, :].T.astype(o_ref.dtype)


def _w_in_section(w_cols, depth, col0, ncols, cb):
    _, rows, _ = w_cols.shape
    d = rows // depth * LANES
    return pl.pallas_call(
        functools.partial(_w_in_section_kernel, depth=depth),
        grid=(ncols // cb,),
        in_specs=[pl.BlockSpec((pl.Element(cb), pl.Element(rows), pl.Element(LANES)),
                               lambda i: (col0 + i * cb, 0, 0))],
        out_specs=pl.BlockSpec((depth, d, cb), lambda i: (0, 0, i)),
        out_shape=jax.ShapeDtypeStruct((depth, d, ncols), BF16),
        compiler_params=_cparams(("parallel",)),
        name="w_in_section",
    )(w_cols)


def _norm_proj_kernel(x_ref, g_ref, wa_ref, wd_ref, cs_ref, fhi_ref, flo_ref, fb_ref,
                      o_ref, g1_ref, g2_ref, logf_ref, stage_ref, *, tn):
    tm = x_ref.shape[0]
    h32 = _rms(x_ref[...], g_ref[...])
    h, h_lo = _split2(h32)
    f_hh_hl = _dot(h, jnp.concatenate([fhi_ref[...], flo_ref[...]], axis=1))
    y = f_hh_hl[:, :LANES] + f_hh_hl[:, LANES:] + _dot(h_lo, fhi_ref[...]) + fb_ref[...]
    logf_ref[...] = jnp.minimum(y, 0.0) - jnp.log(1.0 + jnp.exp(-jnp.abs(y)))
    na = wa_ref.shape[1] // tn
    for c in range(na + wd_ref.shape[1] // tn):
        w = wa_ref[:, c * tn:(c + 1) * tn] if c < na else wd_ref[:, (c - na) * tn:(c - na + 1) * tn]
        cols = slice(c * tn, (c + 1) * tn)
        res = _dot(h, w) * cs_ref[:, cols]
        o_ref[:, cols] = res.astype(o_ref.dtype)
        kind = c - OFF_QB // tn
        if 0 <= kind < 3:
            per = BW // LANES
            for j in range((N_GROUPS - 1) * per):
                stage_ref[j] = res[:, BW + j * LANES:BW + (j + 1) * LANES]
            for gi, out_ref in enumerate((g1_ref, g2_ref)):
                dil = DIL_PATTERNS[gi + 1][1]
                for cl in range(dil):
                    for j in range(per):
                        rows = stage_ref[gi * per + j, pl.ds(cl, tm // dil, stride=dil), :]
                        lanes = slice(kind * BW + j * LANES, kind * BW + (j + 1) * LANES)
                        out_ref[0, cl, :, lanes] = rows.astype(out_ref.dtype)


def _norm_proj(x2d, g, w_a, w_d, wf_hi, wf_lo, layer, colscale, bf, bsz, tm):
    n, d = x2d.shape
    s = n // bsz
    per_batch = s // tm
    tn = N_GROUPS * BW
    resident = pl.Buffered(1)
    d1, d2 = DIL_PATTERNS[1][1], DIL_PATTERNS[2][1]
    return pl.pallas_call(
        functools.partial(_norm_proj_kernel, tn=tn),
        grid=(n // tm,),
        in_specs=[
            pl.BlockSpec((tm, d), lambda i: (i, 0)),
            pl.BlockSpec((1, d), lambda i: (0, 0)),
            pl.BlockSpec((None,) + w_a.shape[1:], lambda i: (layer, 0, 0), pipeline_mode=resident),
            pl.BlockSpec((None,) + w_d.shape[1:], lambda i: (layer, 0, 0), pipeline_mode=resident),
            pl.BlockSpec((1, W_MAIN), lambda i: (0, 0)),
            pl.BlockSpec((None, d, LANES), lambda i: (layer, 0, 0)),
            pl.BlockSpec((None, d, LANES), lambda i: (layer, 0, 0)),
            pl.BlockSpec((1, LANES), lambda i: (0, 0)),
        ],
        out_specs=[
            pl.BlockSpec((tm, W_MAIN), lambda i: (i, 0)),
            pl.BlockSpec((1, d1, tm // d1, tn), lambda i: (i // per_batch, 0, i % per_batch, 0)),
            pl.BlockSpec((1, d2, tm // d2, tn), lambda i: (i // per_batch, 0, i % per_batch, 0)),
            pl.BlockSpec((tm, LANES), lambda i: (i, 0)),
        ],
        out_shape=[
            jax.ShapeDtypeStruct((n, W_MAIN), BF16),
            jax.ShapeDtypeStruct((bsz, d1, s // d1, tn), BF16),
            jax.ShapeDtypeStruct((bsz, d2, s // d2, tn), BF16),
            jax.ShapeDtypeStruct((n, LANES), F32),
        ],
        scratch_shapes=[pltpu.VMEM(((N_GROUPS - 1) * BW // LANES, tm, LANES), F32)],
        compiler_params=_cparams(("parallel",)),
        name="norm_proj",
    )(x2d, g, w_a, w_d, colscale, wf_hi, wf_lo, bf)


FOX_PIECES = 3


def _fox_placement():
    m = np.zeros((FOX_PIECES, LANES, N_HEADS * LANES), np.float32)
    for hh in range(N_HEADS):
        spare = HEAD_DIM if hh % 2 == 0 else 0
        for i in range(FOX_PIECES):
            m[i, hh, hh * LANES + spare + i] = 1.0
    return m


def _cumsum_kernel(l_ref, tril_ref, place_ref, o_ref):
    tril = tril_ref[...]
    blk = tril.shape[0]
    carry = jnp.zeros((1, LANES), F32)
    for j in range(l_ref.shape[1] // blk):
        l1, l2, l3 = _split3(l_ref[0, j * blk:(j + 1) * blk, :])
        c = (_dot(tril, l1.astype(BF16)) + _dot(tril, l2.astype(BF16))
             + _dot(tril, l3.astype(BF16))) + carry
        carry = c[blk - 1:blk, :]
        tiles = None
        for i, piece in enumerate(_split3(c * (-LOG2E))):
            t = _dot(piece.astype(BF16), place_ref[i])
            tiles = t if tiles is None else tiles + t
        o_ref[0, j * blk:(j + 1) * blk, :] = tiles.astype(o_ref.dtype)


def _forget_cum(logf, tril, place, bsz):
    s = logf.shape[0] // bsz
    return pl.pallas_call(
        _cumsum_kernel,
        grid=(bsz,),
        in_specs=[
            pl.BlockSpec((1, s, LANES), lambda i: (i, 0, 0)),
            pl.BlockSpec(tril.shape, lambda i: (0, 0)),
            pl.BlockSpec(place.shape, lambda i: (0, 0, 0)),
        ],
        out_specs=pl.BlockSpec((1, s, N_HEADS * LANES), lambda i: (i, 0, 0)),
        out_shape=jax.ShapeDtypeStruct((bsz, s, N_HEADS * LANES), BF16),
        compiler_params=_cparams(("parallel",)),
        name="forget_cumsum",
    )(logf.reshape(bsz, s, LANES), tril, place)


def _ones_at(lane, lane0, n):
    return jnp.where((lane >= lane0) & (lane < lane0 + n), 1.0, 0.0)


def _lane_masks(lo, hi, spare, n_ones):
    lane = lax.broadcasted_iota(jnp.int32, (1, LANES), 1)
    own_l = (lane >= lo) & (lane < hi)
    return (jnp.where(own_l, 1.0, 0.0).astype(BF16), jnp.where(own_l, 0.0, 1.0).astype(BF16),
            _ones_at(lane, spare, n_ones).astype(BF16))


def _normalize_pair(acc0, acc1, lane):
    den0 = pltpu.roll(acc0, HEAD_DIM, 1)
    den1 = pltpu.roll(acc1, HEAD_DIM, 1)
    return jnp.where(lane < HEAD_DIM, acc0 / den0, acc1 / den1)


def _two_pass_attention(qa, ka_ref, va_ref, s_ref, v_of_map, i, tq, finish):
    n_maps = len(qa)
    tk = tq
    hq = tq // 2
    nq = ka_ref.shape[1] // tk
    row = lax.broadcasted_iota(jnp.int32, (hq, hq), 0)
    col = lax.broadcasted_iota(jnp.int32, (hq, hq), 1)
    causal = col <= row

    def block(k):
        mcol = [None] * n_maps
        for c in range(k):
            for m in range(n_maps):
                s = _dot_nt(qa[m], ka_ref[m, c * tk:(c + 1) * tk, :])
                s_ref[m, c] = s
                fm = _fold_max(s)
                mcol[m] = fm if mcol[m] is None else jnp.maximum(mcol[m], fm)
        d0 = k * tk
        for m in range(n_maps):
            s_top = jnp.where(causal, _dot_nt(qa[m][:hq], ka_ref[m, d0:d0 + hq, :]), NEG)
            s_bot = _dot_nt(qa[m][hq:], ka_ref[m, d0:d0 + tk, :])
            s_bot = jnp.concatenate([s_bot[:, :hq], jnp.where(causal, s_bot[:, hq:], NEG)], axis=1)
            s_ref[m, k, :hq, :hq] = s_top
            s_ref[m, k, hq:, :] = s_bot
            fm = jnp.concatenate([_fold_max(s_top), _fold_max(s_bot)], axis=0)
            mcol[m] = fm if mcol[m] is None else jnp.maximum(mcol[m], fm)
        accs = []
        for m in range(n_maps):
            mrow = jnp.broadcast_to(jnp.max(mcol[m], axis=1, keepdims=True), (tq, LANES))
            mb = jnp.concatenate([mrow] * (tk // LANES), axis=1)
            va = va_ref.at[v_of_map[m]]
            p_top = jnp.exp2(s_ref[m, k, :hq, :hq] - mb[:hq, :hq]).astype(BF16)
            p_bot = jnp.exp2(s_ref[m, k, hq:, :] - mb[hq:]).astype(BF16)
            acc = jnp.concatenate([_dot(p_top, va[d0:d0 + hq, :]),
                                   _dot(p_bot, va[d0:d0 + tk, :])], axis=0)
            for c in range(k):
                pr = jnp.exp2(s_ref[m, c] - mb).astype(BF16)
                acc = acc + _dot(pr, va[c * tk:(c + 1) * tk, :])
            accs.append(acc)
        finish(accs)

    for k in range(nq):
        pl.when(i == k)(functools.partial(block, k))


def _fox_kernel(q_ref, k_ref, v_ref, bias_ref, o_ref, ka_ref, va_ref, s_ref, *, tq):
    i = pl.program_id(2)
    masks = [_lane_masks(h * HEAD_DIM, (h + 1) * HEAD_DIM, HEAD_DIM if h == 0 else 0, FOX_PIECES)
             for h in range(2)]

    @pl.when(i == 0)
    def _prep():
        for h, (own, other, _) in enumerate(masks):
            ka_ref[h] = k_ref[0] * own + bias_ref[0, :, h * LANES:(h + 1) * LANES]
            va_ref[h] = v_ref[0] * own + other

    lane_q = lax.broadcasted_iota(jnp.int32, (tq, LANES), 1)
    qa = [q_ref[0] * own + ones for own, _, ones in masks]

    def finish(accs):
        o_ref[0] = _normalize_pair(accs[0], accs[1], lane_q).astype(o_ref.dtype)

    _two_pass_attention(qa, ka_ref, va_ref, s_ref, (0, 1), i, tq, finish)


def _fox_attn(proj, bias, tq):
    b, s, _ = proj.shape
    nq = s // tq
    qc, kc, vc = OFF_QC // LANES, OFF_KC // LANES, OFF_VC // LANES
    return pl.pallas_call(
        functools.partial(_fox_kernel, tq=tq),
        grid=(b, 2, nq),
        in_specs=[
            pl.BlockSpec((1, tq, LANES), lambda bi, p, i: (bi, i, qc + p)),
            pl.BlockSpec((1, s, LANES), lambda bi, p, i: (bi, 0, kc + p)),
            pl.BlockSpec((1, s, LANES), lambda bi, p, i: (bi, 0, vc + p)),
            pl.BlockSpec((1, s, 2 * LANES), lambda bi, p, i: (bi, 0, p)),
        ],
        out_specs=pl.BlockSpec((1, tq, LANES), lambda bi, p, i: (bi, i, p)),
        out_shape=jax.ShapeDtypeStruct((b, s, BW), BF16),
        scratch_shapes=[
            pltpu.VMEM((2, s, LANES), BF16),
            pltpu.VMEM((2, s, LANES), BF16),
            pltpu.VMEM((2, nq, tq, tq), F32),
        ],
        compiler_params=_cparams(("parallel", "parallel", "arbitrary")),
        name="fox_attn",
    )(proj, proj, proj, bias)


def _diff_bias_table(seq, slopes):
    out = np.zeros((N_HEADS, 2, seq, LANES), np.float32)
    for hh in range(N_HEADS):
        for c in range(2):
            spare = ((2 * (hh % 2) + c + 1) % 4) * DIFF_QK_DIM
            rest = np.arange(seq, dtype=np.float64) * (slopes[hh] * LOG2E)
            for i in range(N_PIECES):
                piece = rest.astype(ml_dtypes.bfloat16).astype(np.float64)
                out[hh, c, :, spare + i] = piece
                rest = rest - piece
    return out.astype(ml_dtypes.bfloat16)


def _diff_kernel(q_ref, k_ref, v_ref, bias_ref, lam_ref, g_ref, o_ref, ka_ref, va_ref, s_ref, *,
                 tq, out_scale, lambda_init):
    i = pl.program_id(2)
    dq = DIFF_QK_DIM
    map_masks = [_lane_masks(m * dq, (m + 1) * dq, ((m + 1) % 4) * dq, N_PIECES) for m in range(4)]

    @pl.when(i == 0)
    def _prep():
        for m, (own, _, _) in enumerate(map_masks):
            ka_ref[m] = k_ref[0] * own + bias_ref[m // 2, m % 2]
        for h in range(2):
            own, other, _ = _lane_masks(h * HEAD_DIM, (h + 1) * HEAD_DIM, 0, 0)
            va_ref[h] = v_ref[0] * own + other

    lane_q = lax.broadcasted_iota(jnp.int32, (tq, LANES), 1)
    qa = [q_ref[0] * own + ones for own, _, ones in map_masks]

    lp = lam_ref[...]
    lam = (jnp.exp(jnp.sum(lp[0:1] * lp[1:2], axis=1, keepdims=True))
           - jnp.exp(jnp.sum(lp[2:3] * lp[3:4], axis=1, keepdims=True)) + lambda_init)
    r = lax.broadcasted_iota(jnp.int32, (LANES, LANES), 0) // HEAD_DIM
    cc = lax.broadcasted_iota(jnp.int32, (LANES, LANES), 1) // HEAD_DIM
    same = jnp.where(r == cc, 1.0, 0.0).astype(BF16)

    def finish(accs):
        o1 = _normalize_pair(accs[0], accs[2], lane_q)
        o2 = _normalize_pair(accs[1], accs[3], lane_q)
        o = o1 - lam * o2
        o2_hi, o2_lo = _split2(o * o)
        ms = (_dot(o2_hi, same) + _dot(o2_lo, same)) * (1.0 / HEAD_DIM)
        y = o * lax.rsqrt(ms + RMS_EPS) * g_ref[...]
        o_ref[0] = (y * out_scale).astype(o_ref.dtype)

    _two_pass_attention(qa, ka_ref, va_ref, s_ref, (0, 0, 1, 1), i, tq, finish)


def _diff_attn(proj, bias, lam_params, g2, tq, out_scale, lambda_init):
    b, s, _ = proj.shape
    nq = s // tq
    qc, kc, vc = OFF_QD // LANES, OFF_KD // LANES, OFF_VD // LANES
    return pl.pallas_call(
        functools.partial(_diff_kernel, tq=tq, out_scale=out_scale, lambda_init=lambda_init),
        grid=(b, 2, nq),
        in_specs=[
            pl.BlockSpec((1, tq, LANES), lambda bi, p, i: (bi, i, qc + p)),
            pl.BlockSpec((1, s, LANES), lambda bi, p, i: (bi, 0, kc + p)),
            pl.BlockSpec((1, s, LANES), lambda bi, p, i: (bi, 0, vc + p)),
            pl.BlockSpec((2, 2, s, LANES), lambda bi, p, i: (p, 0, 0, 0)),
            pl.BlockSpec((4, DIFF_QK_DIM), lambda bi, p, i: (0, 0)),
            pl.BlockSpec((1, LANES), lambda bi, p, i: (0, 0)),
        ],
        out_specs=pl.BlockSpec((1, tq, LANES), lambda bi, p, i: (bi, i, p)),
        out_shape=jax.ShapeDtypeStruct((b, s, BW), BF16),
        scratch_shapes=[
            pltpu.VMEM((4, s, LANES), BF16),
            pltpu.VMEM((2, s, LANES), BF16),
            pltpu.VMEM((4, nq, tq, tq), F32),
        ],
        compiler_params=_cparams(("parallel", "parallel", "arbitrary")),
        name="diff_attn",
    )(proj, proj, proj, bias, lam_params, g2)


def _dil_masks():
    t = np.arange(DB)[:, None]
    s = np.arange(DB)[None, :]
    first = np.where(s <= t, 0.0, NEG)
    s2 = np.arange(2 * DB)[None, :]
    dist = t + DB - s2
    band = np.where((dist >= 0) & (dist <= DB), 0.0, NEG)
    return first.astype(np.float32), band.astype(np.float32)


def _dil_bias_table(seq, slopes):
    out = np.zeros((N_GROUPS, N_HEADS, seq, LANES), np.float32)
    rho = np.arange(seq)
    for g, (window, dil) in enumerate(DIL_PATTERNS):
        cls_len = seq // dil
        pos = (rho % cls_len) * dil + rho // cls_len
        for hh in range(N_HEADS):
            spare = HEAD_DIM if hh % 2 == 0 else 0
            rest = pos.astype(np.float64) * (slopes[g][hh] * LOG2E)
            for i in range(N_PIECES):
                piece = rest.astype(ml_dtypes.bfloat16).astype(np.float64)
                out[g, hh, :, spare + i] = piece
                rest = rest - piece
    return out.astype(ml_dtypes.bfloat16)


def _dil_kernel(*refs, slopes):
    q_refs, k_refs, v_refs = refs[0:3], refs[3:6], refs[6:9]
    (bias_ref, first_ref, band_ref, o_ref, qa_ref, ka_ref, va_ref, og_ref, lse_ref,
     s_ref, m_ref, acc_ref) = refs[9:]
    p = pl.program_id(0)
    s_len = k_refs[0].shape[1]
    n_blk = s_len // DB

    def slope_of(g, h):
        return jnp.where(p == 0, slopes[g][h], slopes[g][2 + h]).astype(F32)

    lane1 = lax.broadcasted_iota(jnp.int32, (1, LANES), 1)
    for h in range(2):
        own_l = (lane1 < HEAD_DIM) if h == 0 else (lane1 >= HEAD_DIM)
        spare = HEAD_DIM if h == 0 else 0
        own = jnp.where(own_l, 1.0, 0.0).astype(BF16)
        other = jnp.where(own_l, 0.0, 1.0).astype(BF16)
        ones = _ones_at(lane1, spare, N_PIECES).astype(BF16)
        for g in range(N_GROUPS):
            qa_ref[2 * g + h] = q_refs[g][0] * own + ones
            ka_ref[2 * g + h] = k_refs[g][0] * own + bias_ref[g, h]
            va_ref[2 * g + h] = v_refs[g][0] * own + other

    low = lax.broadcasted_iota(jnp.int32, (s_len, LANES), 1) < HEAD_DIM
    for g, (window, dil) in enumerate(DIL_PATTERNS):
        cls_blocks = n_blk // dil
        cls_len = s_len // dil

        def window_of(j):
            first = j % cls_blocks == 0
            return (j * DB if first else (j - 1) * DB), (DB if first else 2 * DB), first

        for j in range(n_blk):
            k0, kw, first = window_of(j)
            mask = first_ref[...] if first else band_ref[...]
            for h in range(2):
                gh = 2 * g + h
                s = _dot_nt(qa_ref[gh, j * DB:(j + 1) * DB, :], ka_ref[gh, k0:k0 + kw, :]) + mask
                s_ref[h, j, :, :kw] = s
                m = jnp.max(s, axis=1, keepdims=True)
                m_ref[h, j * DB:(j + 1) * DB, :] = jnp.broadcast_to(m, (DB, LANES))
        for j in range(n_blk):
            k0, kw, first = window_of(j)
            for h in range(2):
                gh = 2 * g + h
                m = m_ref[h, j * DB:(j + 1) * DB, :]
                mb = m if first else jnp.concatenate([m, m], axis=1)
                pr = jnp.exp2(s_ref[h, j, :, :kw] - mb).astype(BF16)
                acc_ref[h, j * DB:(j + 1) * DB, :] = _dot(pr, va_ref[gh, k0:k0 + kw, :])
        acc0, acc1 = acc_ref[0], acc_ref[1]
        num = jnp.where(low, acc0, acc1)
        den = jnp.where(low, pltpu.roll(acc0, HEAD_DIM, 1), pltpu.roll(acc1, HEAD_DIM, 1))
        out_g = num / den
        lse_g = jnp.where(low, m_ref[0], m_ref[1]) + jnp.log(den) * LOG2E
        for c in range(dil):
            rows = pl.ds(c, cls_len, stride=dil) if dil > 1 else pl.ds(0, cls_len)
            og_ref[g, rows, :] = out_g[c * cls_len:(c + 1) * cls_len]
            lse_ref[g, rows, :] = lse_g[c * cls_len:(c + 1) * cls_len]

    lane_s = lax.broadcasted_iota(jnp.int32, (s_len, LANES), 1)
    t = lax.broadcasted_iota(jnp.int32, (s_len, LANES), 0).astype(F32)
    lses = []
    for g in range(N_GROUPS):
        slope_lane = jnp.where(lane_s < HEAD_DIM, slope_of(g, 0), slope_of(g, 1)) * LOG2E
        lses.append(lse_ref[g] - slope_lane * t)
    top = jnp.maximum(jnp.maximum(lses[0], lses[1]), lses[2])
    wsum = jnp.zeros((s_len, LANES), F32)
    out = jnp.zeros((s_len, LANES), F32)
    for g in range(N_GROUPS):
        w = jnp.exp2(lses[g] - top)
        wsum = wsum + w
        out = out + w * og_ref[g]
    o_ref[0] = (out / wsum).astype(o_ref.dtype)


def _dil_attn(proj, g1, g2, bias, first, band, slopes):
    b, s, _ = proj.shape
    per = BW // LANES

    def spec(blk):
        return pl.BlockSpec((1, s, LANES), lambda p, bi: (bi, 0, blk + p))

    arrays, in_specs = [], []
    for kind, off in enumerate((OFF_QB, OFF_KB, OFF_VB)):
        arrays += [proj, g1, g2]
        in_specs += [spec(off // LANES), spec(kind * per), spec(kind * per)]
    in_specs += [pl.BlockSpec((N_GROUPS, 2, s, LANES), lambda p, bi: (0, p, 0, 0)),
                 pl.BlockSpec(first.shape, lambda p, bi: (0, 0)),
                 pl.BlockSpec(band.shape, lambda p, bi: (0, 0))]
    return pl.pallas_call(
        functools.partial(_dil_kernel, slopes=slopes),
        grid=(2, b),
        in_specs=in_specs,
        out_specs=pl.BlockSpec((1, s, LANES), lambda p, bi: (bi, 0, p)),
        out_shape=jax.ShapeDtypeStruct((b, s, BW), BF16),
        scratch_shapes=[
            pltpu.VMEM((2 * N_GROUPS, s, LANES), BF16),
            pltpu.VMEM((2 * N_GROUPS, s, LANES), BF16),
            pltpu.VMEM((2 * N_GROUPS, s, LANES), BF16),
            pltpu.VMEM((N_GROUPS, s, LANES), F32),
            pltpu.VMEM((N_GROUPS, s, LANES), F32),
            pltpu.VMEM((2, s // DB, DB, 2 * DB), F32),
            pltpu.VMEM((2, s, LANES), F32),
            pltpu.VMEM((2, s, LANES), F32),
        ],
        compiler_params=_cparams(("parallel", "parallel")),
        name="dil_attn",
    )(*arrays, bias, first, band)


def _sb_kernel(q_ref, k_ref, v_ref, o_ref, *, tq, tk):
    i = pl.program_id(2)
    n_diag = tq // tk
    lane_q = lax.broadcasted_iota(jnp.int32, (tq, LANES), 1)
    q = q_ref[0].astype(F32)
    row = lax.broadcasted_iota(jnp.int32, (tq, tk), 0)
    col = lax.broadcasted_iota(jnp.int32, (tq, tk), 1)
    tr = lax.broadcasted_iota(jnp.int32, (tk, tk), 0)
    tc = lax.broadcasted_iota(jnp.int32, (tk, tk), 1)
    tri = jnp.where(tr >= tc, 1.0, 0.0).astype(BF16)
    qn = [jnp.where((lane_q < HEAD_DIM) if h == 0 else (lane_q >= HEAD_DIM), -q, 0.0).astype(BF16)
          for h in range(2)]

    def scores(h, c, strict):
        kc = k_ref[0, c * tk:(c + 1) * tk, :]
        zn = _dot_nt(qn[h], kc)
        lk = jnp.minimum(zn, 0.0) - jnp.log(1.0 + jnp.exp(-jnp.abs(zn)))
        if strict is not None:
            lk = jnp.where(strict, lk, 0.0)
        return zn, _dot(lk.astype(BF16), tri)

    def weigh(h, c, strict, zn, suffix, run, acc):
        vc = v_ref[0, c * tk:(c + 1) * tk, :]
        run_b = jnp.concatenate([run] * (tk // LANES), axis=1)
        a = jnp.exp(suffix + run_b - zn)
        if strict is not None:
            a = jnp.where(strict, a, 0.0)
        acc = acc + _dot(a.astype(BF16), vc)
        return run + jnp.broadcast_to(suffix[:, 0:1], (tq, LANES)), acc

    def step(chunks, stricts, carry):
        parts = [[scores(h, c, st) for c, st in zip(chunks, stricts)] for h in range(2)]
        out = []
        for h in range(2):
            run, acc = carry[2 * h], carry[2 * h + 1]
            for c, st, (zn, suffix) in zip(chunks, stricts, parts[h]):
                run, acc = weigh(h, c, st, zn, suffix, run, acc)
            out += [run, acc]
        return tuple(out)

    def block(k):
        carry = tuple(jnp.zeros((tq, LANES), F32) for _ in range(4))
        diag = list(range(n_diag - 1, -1, -1))
        carry = step([k * n_diag + d for d in diag], [col + d * tk < row for d in diag], carry)
        for n in range(k):
            newest = k * n_diag - 1 - n * n_diag
            carry = step([newest - d for d in range(n_diag)], [None] * n_diag, carry)
        o_ref[0] = jnp.where(lane_q < HEAD_DIM, carry[1], carry[3]).astype(o_ref.dtype)

    for k in range(k_ref.shape[1] // tq):
        pl.when(i == k)(functools.partial(block, k))


def _sb_attn(proj, tq, tk):
    b, s, _ = proj.shape
    nq = s // tq
    qc, kc, vc = OFF_QA // LANES, OFF_KA // LANES, OFF_VA // LANES
    return pl.pallas_call(
        functools.partial(_sb_kernel, tq=tq, tk=tk),
        grid=(b, 2, nq),
        in_specs=[
            pl.BlockSpec((1, tq, LANES), lambda bi, p, i: (bi, i, qc + p)),
            pl.BlockSpec((1, s, LANES), lambda bi, p, i: (bi, 0, kc + p)),
            pl.BlockSpec((1, s, LANES), lambda bi, p, i: (bi, 0, vc + p)),
        ],
        out_specs=pl.BlockSpec((1, tq, LANES), lambda bi, p, i: (bi, i, p)),
        out_shape=jax.ShapeDtypeStruct((b, s, BW), BF16),
        compiler_params=_cparams(("parallel", "parallel", "arbitrary")),
        name="sb_attn",
    )(proj, proj, proj)


def _merge_kernel(x_ref, g_ref, ya_ref, yb_ref, yc_ref, yd_ref, wg_ref, wb_ref, wo_ref, o_ref):
    x = x_ref[...]
    h = _rms(x, g_ref[...]).astype(BF16)
    merged = None
    for n, y_ref in enumerate((ya_ref, yb_ref, yc_ref, yd_ref)):
        br = _dot(y_ref[...], wb_ref[n])
        gl = _dot(h, wg_ref[:, n * D_MODEL:(n + 1) * D_MODEL])
        term = br * (0.5 + 0.5 * jnp.tanh(0.5 * gl))
        merged = term if merged is None else merged + term
    o_ref[...] = x + _dot(merged.astype(BF16), wo_ref[...])


def _merge_out(x2d, g, ys, wg, wb, wo, layer, tm):
    n, d = x2d.shape
    resident = pl.Buffered(1)
    return pl.pallas_call(
        _merge_kernel,
        grid=(n // tm,),
        in_specs=[pl.BlockSpec((tm, d), lambda i: (i, 0)),
                  pl.BlockSpec((1, d), lambda i: (0, 0))]
        + [pl.BlockSpec((tm, BW), lambda i: (i, 0))] * 4
        + [
            pl.BlockSpec((None, d, GATE_W), lambda i: (layer, 0, 0), pipeline_mode=resident),
            pl.BlockSpec((None, 4, BW, d), lambda i: (layer, 0, 0, 0), pipeline_mode=resident),
            pl.BlockSpec((None, d, d), lambda i: (layer, 0, 0), pipeline_mode=resident),
        ],
        out_specs=pl.BlockSpec((tm, d), lambda i: (i, 0)),
        out_shape=jax.ShapeDtypeStruct((n, d), F32),
        compiler_params=_cparams(("parallel",)),
        name="merge_out",
    )(x2d, g, *ys, wg, wb, wo)


def _mlp_kernel(x_ref, g_ref, wu_ref, wd_ref, gf_ref, o_ref, *, final_norm, tf):
    x = x_ref[...]
    h = _rms(x, g_ref[...]).astype(BF16)
    y = x
    for c in range(wu_ref.shape[1] // tf):
        up = _dot(h, wu_ref[:, c * tf:(c + 1) * tf])
        act = jnp.square(jnp.maximum(up, 0.0)).astype(BF16)
        y = y + _dot(act, wd_ref[c * tf:(c + 1) * tf, :])
    if final_norm:
        y = _rms(y, gf_ref[...])
    o_ref[...] = y


def _mlp(x2d, g, wu, wd, layer, gf, tm, tf, final_norm):
    n, d = x2d.shape
    dff = wu.shape[2]
    resident = pl.Buffered(1)
    return pl.pallas_call(
        functools.partial(_mlp_kernel, final_norm=final_norm, tf=tf),
        grid=(n // tm,),
        in_specs=[
            pl.BlockSpec((tm, d), lambda i: (i, 0)),
            pl.BlockSpec((1, d), lambda i: (0, 0)),
            pl.BlockSpec((None, d, dff), lambda i: (layer, 0, 0), pipeline_mode=resident),
            pl.BlockSpec((None, dff, d), lambda i: (layer, 0, 0), pipeline_mode=resident),
            pl.BlockSpec((1, d), lambda i: (0, 0)),
        ],
        out_specs=pl.BlockSpec((tm, d), lambda i: (i, 0)),
        out_shape=jax.ShapeDtypeStruct((n, d), F32),
        compiler_params=_cparams(("parallel",)),
        name="mlp",
    )(x2d, g, wu, wd, gf)


def _col_scale():
    cs = np.ones((1, W_MAIN), np.float32)
    cs[0, OFF_QA:OFF_QA + BW] = HEAD_DIM ** -0.5
    cs[0, OFF_QB:OFF_QB + N_GROUPS * BW] = HEAD_DIM ** -0.5 * LOG2E
    cs[0, OFF_QC:OFF_QC + BW] = HEAD_DIM ** -0.5 * LOG2E
    cs[0, OFF_QD:OFF_QD + BW] = DIFF_QK_DIM ** -0.5 * LOG2E
    return cs


def kernel(x, mix_norm_g, w_in, b_forget, lambda_q1, lambda_k1, lambda_q2, lambda_k2,
           diff_norm_g, w_branch, w_out, mlp_norm_g, w_up, w_down, final_norm_g):
    bsz, seq, d = x.shape
    depth = w_in.shape[0]
    n = bsz * seq
    tq_big = min(512, seq)
    tm = min(1024, n)

    slopes = [float(v) for v in _alibi_slopes()]
    n0, n1 = N_HEADS, 2 * N_HEADS
    slopes_dil = (tuple(slopes[:n0]), tuple(slopes[n1:n1 + N_HEADS]), tuple(slopes[n1 + N_HEADS:]))
    slopes_diff = tuple(slopes[n0:n1])

    assert all(w // r == DB and seq % (DB * r) == 0 for w, r in DIL_PATTERNS) and n % tm == 0
    first, band = (jnp.asarray(m) for m in _dil_masks())
    dil_bias = jnp.asarray(_dil_bias_table(seq, slopes_dil))
    diff_bias = jnp.asarray(_diff_bias_table(seq, slopes_diff))
    fox_place = jnp.asarray(_fox_placement(), BF16)
    tril = jnp.asarray(np.tril(np.ones((2 * LANES, 2 * LANES), np.float32)), BF16)
    colscale = jnp.asarray(_col_scale())

    d_end = F_COL + N_HEADS + 3 * BW
    w_cols = (w_in.reshape(depth, d // LANES, LANES, w_in.shape[2]).transpose(3, 1, 0, 2)
              .reshape(w_in.shape[2], d // LANES * depth, LANES))
    w_a = _w_in_section(w_cols, depth, 0, F_COL, 256)
    w_d = _w_in_section(w_cols, depth, F_COL + N_HEADS, 3 * BW, 256)
    w_g = _w_in_section(w_cols, depth, d_end, GATE_W, 256)
    wf = jnp.pad(w_in[:, :, F_COL:F_COL + N_HEADS], ((0, 0), (0, 0), (0, LANES - N_HEADS)))
    wf_hi = wf.astype(BF16)
    wf_lo = (wf - wf_hi.astype(F32)).astype(BF16)
    w_up_bf = _cast_bf16(w_up.reshape(depth * d, D_FF), 256).reshape(depth, d, D_FF)
    w_down_bf = _cast_bf16(w_down.reshape(depth * D_FF, d), 1024).reshape(depth, D_FF, d)
    w_branch_bf = _cast_bf16(w_branch.reshape(depth * 4 * BW, d), 1024).reshape(depth, 4, BW, d)
    w_out_bf = _cast_bf16(w_out.reshape(depth * d, d), 1024).reshape(depth, d, d)

    x2d = x.reshape(n, d)
    for l in range(depth):
        lambda_init = 0.8 - 0.6 * math.exp(-0.3 * l)
        bf = jnp.pad(b_forget[l].astype(F32), (0, LANES - N_HEADS)).reshape(1, LANES)
        g_mix = mix_norm_g[l].reshape(1, d)

        proj, pg1, pg2, logf = _norm_proj(x2d, g_mix, w_a, w_d, wf_hi, wf_lo, l, colscale, bf, bsz,
                                          min(1024, seq))
        fox_bias = _forget_cum(logf, tril, fox_place, bsz)
        proj3 = proj.reshape(bsz, seq, W_MAIN)

        y_a = _sb_attn(proj3, tq_big, min(256, seq))
        y_b = _dil_attn(proj3, pg1.reshape(bsz, seq, -1), pg2.reshape(bsz, seq, -1), dil_bias,
                        first, band, slopes_dil)
        y_c = _fox_attn(proj3, fox_bias, tq_big)
        lam_params = jnp.stack([lambda_q1[l], lambda_k1[l], lambda_q2[l], lambda_k2[l]]).astype(F32)
        g2 = jnp.tile(diff_norm_g[l].astype(F32), 2).reshape(1, LANES)
        y_d = _diff_attn(proj3, diff_bias, lam_params, g2, tq_big, 1.0 - lambda_init, lambda_init)

        ys = [y.reshape(n, BW) for y in (y_a, y_b, y_c, y_d)]
        x2d = _merge_out(x2d, g_mix, ys, w_g, w_branch_bf, w_out_bf, l, tm)
        x2d = _mlp(x2d, mlp_norm_g[l].reshape(1, d), w_up_bf, w_down_bf, l,
                   final_norm_g.reshape(1, d), tm, 1024,
                   final_norm=(l == depth - 1))
    return x2d.reshape(bsz, seq, d)
```

```python
import functools
import math

import ml_dtypes
import numpy as np
import jax
import jax.numpy as jnp
from jax import lax
from jax.experimental import pallas as pl
from jax.experimental.pallas import tpu as pltpu

F32 = jnp.float32
BF16 = jnp.bfloat16

D_MODEL = 1024
HEAD_DIM = 64
N_HEADS = 4
DIL_PATTERNS = ((128, 1), (512, 4), (2048, 16))
N_GROUPS = len(DIL_PATTERNS)
DIFF_QK_DIM = HEAD_DIM // 2
D_FF = 4 * D_MODEL
RMS_EPS = 1e-6
N_ALIBI = N_GROUPS * N_HEADS + N_HEADS

LANES = 128
BW = N_HEADS * HEAD_DIM
NEG = -1e30
LOG2E = math.log2(math.e)
DB = 128
VMEM_LIMIT = 56 * 1024 * 1024

OFF_QA, OFF_KA, OFF_VA = 0, BW, 2 * BW
OFF_QB, OFF_KB, OFF_VB = 3 * BW, 6 * BW, 9 * BW
OFF_QC, OFF_KC, OFF_VC = 12 * BW, 13 * BW, 14 * BW
OFF_QD, OFF_KD, OFF_VD = 15 * BW, 16 * BW, 17 * BW
W_MAIN = 18 * BW
GATE_W = 4 * D_MODEL
F_COL = 15 * BW
N_PIECES = 4


def _alibi_slopes():
    return 2.0 ** (-8.0 * np.arange(1, N_ALIBI + 1) / N_ALIBI)


def _cparams(sem):
    return pltpu.CompilerParams(dimension_semantics=sem, vmem_limit_bytes=VMEM_LIMIT)


def _rms(x, g):
    ms = jnp.mean(x * x, axis=-1, keepdims=True)
    return x * lax.rsqrt(ms + RMS_EPS) * g


def _split3(val):
    p1 = val.astype(BF16).astype(F32)
    r1 = val - p1
    p2 = r1.astype(BF16).astype(F32)
    p3 = (r1 - p2).astype(BF16).astype(F32)
    return p1, p2, p3


def _split2(val):
    hi = val.astype(BF16)
    lo = (val - hi.astype(F32)).astype(BF16)
    return hi, lo


def _dot_nt(a, b):
    return lax.dot_general(a, b, (((1,), (1,)), ((), ())), preferred_element_type=F32)


def _dot(a, b):
    return jnp.dot(a, b, preferred_element_type=F32)


def _fold_max(s):
    m = s[:, :LANES]
    for j in range(1, s.shape[1] // LANES):
        m = jnp.maximum(m, s[:, j * LANES:(j + 1) * LANES])
    return m


def _cast_kernel(x_ref, o_ref):
    o_ref[...] = x_ref[...].astype(o_ref.dtype)


def _cast_bf16(w2d, tr):
    r, c = w2d.shape
    return pl.pallas_call(
        _cast_kernel,
        grid=(r // tr,),
        in_specs=[pl.BlockSpec((tr, c), lambda i: (i, 0))],
        out_specs=pl.BlockSpec((tr, c), lambda i: (i, 0)),
        out_shape=jax.ShapeDtypeStruct((r, c), BF16),
        compiler_params=_cparams(("parallel",)),
        name="cast_bf16",
    )(w2d)


def _w_in_section_kernel(a_ref, o_ref, *, depth):
    for l in range(depth):
        for j in range(a_ref.shape[1] // depth):
            o_ref[l, j * LANES:(j + 1) * LANES, :] = a_ref[:, j * depth + l, :].T.astype(o_ref.dtype)


def _w_in_section(w_cols, depth, col0, ncols, cb):
    _, rows, _ = w_cols.shape
    d = rows // depth * LANES
    return pl.pallas_call(
        functools.partial(_w_in_section_kernel, depth=depth),
        grid=(ncols // cb,),
        in_specs=[pl.BlockSpec((pl.Element(cb), pl.Element(rows), pl.Element(LANES)),
                               lambda i: (col0 + i * cb, 0, 0))],
        out_specs=pl.BlockSpec((depth, d, cb), lambda i: (0, 0, i)),
        out_shape=jax.ShapeDtypeStruct((depth, d, ncols), BF16),
        compiler_params=_cparams(("parallel",)),
        name="w_in_section",
    )(w_cols)


def _norm_proj_kernel(x_ref, g_ref, wa_ref, wd_ref, cs_ref, fhi_ref, flo_ref, fb_ref,
                      o_ref, g1_ref, g2_ref, logf_ref, stage_ref, *, tn):
    tm = x_ref.shape[0]
    h32 = _rms(x_ref[...], g_ref[...])
    h, h_lo = _split2(h32)
    f_hh_hl = _dot(h, jnp.concatenate([fhi_ref[...], flo_ref[...]], axis=1))
    y = f_hh_hl[:, :LANES] + f_hh_hl[:, LANES:] + _dot(h_lo, fhi_ref[...]) + fb_ref[...]
    logf_ref[...] = jnp.minimum(y, 0.0) - jnp.log(1.0 + jnp.exp(-jnp.abs(y)))
    na = wa_ref.shape[1] // tn
    for c in range(na + wd_ref.shape[1] // tn):
        w = wa_ref[:, c * tn:(c + 1) * tn] if c < na else wd_ref[:, (c - na) * tn:(c - na + 1) * tn]
        cols = slice(c * tn, (c + 1) * tn)
        res = _dot(h, w) * cs_ref[:, cols]
        o_ref[:, cols] = res.astype(o_ref.dtype)
        kind = c - OFF_QB // tn
        if 0 <= kind < 3:
            per = BW // LANES
            for j in range((N_GROUPS - 1) * per):
                stage_ref[j] = res[:, BW + j * LANES:BW + (j + 1) * LANES]
            for gi, out_ref in enumerate((g1_ref, g2_ref)):
                dil = DIL_PATTERNS[gi + 1][1]
                for cl in range(dil):
                    for j in range(per):
                        rows = stage_ref[gi * per + j, pl.ds(cl, tm // dil, stride=dil), :]
                        lanes = slice(kind * BW + j * LANES, kind * BW + (j + 1) * LANES)
                        out_ref[0, cl, :, lanes] = rows.astype(out_ref.dtype)


def _norm_proj(x2d, g, w_a, w_d, wf_hi, wf_lo, layer, colscale, bf, bsz, tm):
    n, d = x2d.shape
    s = n // bsz
    per_batch = s // tm
    tn = N_GROUPS * BW
    resident = pl.Buffered(1)
    d1, d2 = DIL_PATTERNS[1][1], DIL_PATTERNS[2][1]
    return pl.pallas_call(
        functools.partial(_norm_proj_kernel, tn=tn),
        grid=(n // tm,),
        in_specs=[
            pl.BlockSpec((tm, d), lambda i: (i, 0)),
            pl.BlockSpec((1, d), lambda i: (0, 0)),
            pl.BlockSpec((None,) + w_a.shape[1:], lambda i: (layer, 0, 0), pipeline_mode=resident),
            pl.BlockSpec((None,) + w_d.shape[1:], lambda i: (layer, 0, 0), pipeline_mode=resident),
            pl.BlockSpec((1, W_MAIN), lambda i: (0, 0)),
            pl.BlockSpec((None, d, LANES), lambda i: (layer, 0, 0)),
            pl.BlockSpec((None, d, LANES), lambda i: (layer, 0, 0)),
            pl.BlockSpec((1, LANES), lambda i: (0, 0)),
        ],
        out_specs=[
            pl.BlockSpec((tm, W_MAIN), lambda i: (i, 0)),
            pl.BlockSpec((1, d1, tm // d1, tn), lambda i: (i // per_batch, 0, i % per_batch, 0)),
            pl.BlockSpec((1, d2, tm // d2, tn), lambda i: (i // per_batch, 0, i % per_batch, 0)),
            pl.BlockSpec((tm, LANES), lambda i: (i, 0)),
        ],
        out_shape=[
            jax.ShapeDtypeStruct((n, W_MAIN), BF16),
            jax.ShapeDtypeStruct((bsz, d1, s // d1, tn), BF16),
            jax.ShapeDtypeStruct((bsz, d2, s // d2, tn), BF16),
            jax.ShapeDtypeStruct((n, LANES), F32),
        ],
        scratch_shapes=[pltpu.VMEM(((N_GROUPS - 1) * BW // LANES, tm, LANES), F32)],
        compiler_params=_cparams(("parallel",)),
        name="norm_proj",
    )(x2d, g, w_a, w_d, colscale, wf_hi, wf_lo, bf)


FOX_PIECES = 3


def _fox_placement():
    m = np.zeros((FOX_PIECES, LANES, N_HEADS * LANES), np.float32)
    for hh in range(N_HEADS):
        spare = HEAD_DIM if hh % 2 == 0 else 0
        for i in range(FOX_PIECES):
            m[i, hh, hh * LANES + spare + i] = 1.0
    return m


def _cumsum_kernel(l_ref, tril_ref, place_ref, o_ref):
    tril = tril_ref[...]
    blk = tril.shape[0]
    carry = jnp.zeros((1, LANES), F32)
    for j in range(l_ref.shape[1] // blk):
        l1, l2, l3 = _split3(l_ref[0, j * blk:(j + 1) * blk, :])
        c = (_dot(tril, l1.astype(BF16)) + _dot(tril, l2.astype(BF16))
             + _dot(tril, l3.astype(BF16))) + carry
        carry = c[blk - 1:blk, :]
        tiles = None
        for i, piece in enumerate(_split3(c * (-LOG2E))):
            t = _dot(piece.astype(BF16), place_ref[i])
            tiles = t if tiles is None else tiles + t
        o_ref[0, j * blk:(j + 1) * blk, :] = tiles.astype(o_ref.dtype)


def _forget_cum(logf, tril, place, bsz):
    s = logf.shape[0] // bsz
    return pl.pallas_call(
        _cumsum_kernel,
        grid=(bsz,),
        in_specs=[
            pl.BlockSpec((1, s, LANES), lambda i: (i, 0, 0)),
            pl.BlockSpec(tril.shape, lambda i: (0, 0)),
            pl.BlockSpec(place.shape, lambda i: (0, 0, 0)),
        ],
        out_specs=pl.BlockSpec((1, s, N_HEADS * LANES), lambda i: (i, 0, 0)),
        out_shape=jax.ShapeDtypeStruct((bsz, s, N_HEADS * LANES), BF16),
        compiler_params=_cparams(("parallel",)),
        name="forget_cumsum",
    )(logf.reshape(bsz, s, LANES), tril, place)


def _ones_at(lane, lane0, n):
    return jnp.where((lane >= lane0) & (lane < lane0 + n), 1.0, 0.0)


def _lane_masks(lo, hi, spare, n_ones):
    lane = lax.broadcasted_iota(jnp.int32, (1, LANES), 1)
    own_l = (lane >= lo) & (lane < hi)
    return (jnp.where(own_l, 1.0, 0.0).astype(BF16), jnp.where(own_l, 0.0, 1.0).astype(BF16),
            _ones_at(lane, spare, n_ones).astype(BF16))


def _normalize_pair(acc0, acc1, lane):
    den0 = pltpu.roll(acc0, HEAD_DIM, 1)
    den1 = pltpu.roll(acc1, HEAD_DIM, 1)
    return jnp.where(lane < HEAD_DIM, acc0 / den0, acc1 / den1)


def _two_pass_attention(qa, ka_ref, va_ref, s_ref, v_of_map, i, tq, finish):
    n_maps = len(qa)
    tk = tq
    hq = tq // 2
    nq = ka_ref.shape[1] // tk
    row = lax.broadcasted_iota(jnp.int32, (hq, hq), 0)
    col = lax.broadcasted_iota(jnp.int32, (hq, hq), 1)
    causal = col <= row

    def block(k):
        mcol = [None] * n_maps
        for c in range(k):
            for m in range(n_maps):
                s = _dot_nt(qa[m], ka_ref[m, c * tk:(c + 1) * tk, :])
                s_ref[m, c] = s
                fm = _fold_max(s)
                mcol[m] = fm if mcol[m] is None else jnp.maximum(mcol[m], fm)
        d0 = k * tk
        for m in range(n_maps):
            s_top = jnp.where(causal, _dot_nt(qa[m][:hq], ka_ref[m, d0:d0 + hq, :]), NEG)
            s_bot = _dot_nt(qa[m][hq:], ka_ref[m, d0:d0 + tk, :])
            s_bot = jnp.concatenate([s_bot[:, :hq], jnp.where(causal, s_bot[:, hq:], NEG)], axis=1)
            s_ref[m, k, :hq, :hq] = s_top
            s_ref[m, k, hq:, :] = s_bot
            fm = jnp.concatenate([_fold_max(s_top), _fold_max(s_bot)], axis=0)
            mcol[m] = fm if mcol[m] is None else jnp.maximum(mcol[m], fm)
        accs = []
        for m in range(n_maps):
            mrow = jnp.broadcast_to(jnp.max(mcol[m], axis=1, keepdims=True), (tq, LANES))
            mb = jnp.concatenate([mrow] * (tk // LANES), axis=1)
            va = va_ref.at[v_of_map[m]]
            p_top = jnp.exp2(s_ref[m, k, :hq, :hq] - mb[:hq, :hq]).astype(BF16)
            p_bot = jnp.exp2(s_ref[m, k, hq:, :] - mb[hq:]).astype(BF16)
            acc = jnp.concatenate([_dot(p_top, va[d0:d0 + hq, :]),
                                   _dot(p_bot, va[d0:d0 + tk, :])], axis=0)
            for c in range(k):
                pr = jnp.exp2(s_ref[m, c] - mb).astype(BF16)
                acc = acc + _dot(pr, va[c * tk:(c + 1) * tk, :])
            accs.append(acc)
        finish(accs)

    for k in range(nq):
        pl.when(i == k)(functools.partial(block, k))


def _fox_kernel(q_ref, k_ref, v_ref, bias_ref, o_ref, ka_ref, va_ref, s_ref, *, tq):
    i = pl.program_id(2)
    masks = [_lane_masks(h * HEAD_DIM, (h + 1) * HEAD_DIM, HEAD_DIM if h == 0 else 0, FOX_PIECES)
             for h in range(2)]

    @pl.when(i == 0)
    def _prep():
        for h, (own, other, _) in enumerate(masks):
            ka_ref[h] = k_ref[0] * own + bias_ref[0, :, h * LANES:(h + 1) * LANES]
            va_ref[h] = v_ref[0] * own + other

    lane_q = lax.broadcasted_iota(jnp.int32, (tq, LANES), 1)
    qa = [q_ref[0] * own + ones for own, _, ones in masks]

    def finish(accs):
        o_ref[0] = _normalize_pair(accs[0], accs[1], lane_q).astype(o_ref.dtype)

    _two_pass_attention(qa, ka_ref, va_ref, s_ref, (0, 1), i, tq, finish)


def _fox_attn(proj, bias, tq):
    b, s, _ = proj.shape
    nq = s // tq
    qc, kc, vc = OFF_QC // LANES, OFF_KC // LANES, OFF_VC // LANES
    return pl.pallas_call(
        functools.partial(_fox_kernel, tq=tq),
        grid=(b, 2, nq),
        in_specs=[
            pl.BlockSpec((1, tq, LANES), lambda bi, p, i: (bi, i, qc + p)),
            pl.BlockSpec((1, s, LANES), lambda bi, p, i: (bi, 0, kc + p)),
            pl.BlockSpec((1, s, LANES), lambda bi, p, i: (bi, 0, vc + p)),
            pl.BlockSpec((1, s, 2 * LANES), lambda bi, p, i: (bi, 0, p)),
        ],
        out_specs=pl.BlockSpec((1, tq, LANES), lambda bi, p, i: (bi, i, p)),
        out_shape=jax.ShapeDtypeStruct((b, s, BW), BF16),
        scratch_shapes=[
            pltpu.VMEM((2, s, LANES), BF16),
            pltpu.VMEM((2, s, LANES), BF16),
            pltpu.VMEM((2, nq, tq, tq), F32),
        ],
        compiler_params=_cparams(("parallel", "parallel", "arbitrary")),
        name="fox_attn",
    )(proj, proj, proj, bias)


def _diff_bias_table(seq, slopes):
    out = np.zeros((N_HEADS, 2, seq, LANES), np.float32)
    for hh in range(N_HEADS):
        for c in range(2):
            spare = ((2 * (hh % 2) + c + 1) % 4) * DIFF_QK_DIM
            rest = np.arange(seq, dtype=np.float64) * (slopes[hh] * LOG2E)
            for i in range(N_PIECES):
                piece = rest.astype(ml_dtypes.bfloat16).astype(np.float64)
                out[hh, c, :, spare + i] = piece
                rest = rest - piece
    return out.astype(ml_dtypes.bfloat16)


def _diff_kernel(q_ref, k_ref, v_ref, bias_ref, lam_ref, g_ref, o_ref, ka_ref, va_ref, s_ref, *,
                 tq, out_scale, lambda_init):
    i = pl.program_id(2)
    dq = DIFF_QK_DIM
    map_masks = [_lane_masks(m * dq, (m + 1) * dq, ((m + 1) % 4) * dq, N_PIECES) for m in range(4)]

    @pl.when(i == 0)
    def _prep():
        for m, (own, _, _) in enumerate(map_masks):
            ka_ref[m] = k_ref[0] * own + bias_ref[m // 2, m % 2]
        for h in range(2):
            own, other, _ = _lane_masks(h * HEAD_DIM, (h + 1) * HEAD_DIM, 0, 0)
            va_ref[h] = v_ref[0] * own + other

    lane_q = lax.broadcasted_iota(jnp.int32, (tq, LANES), 1)
    qa = [q_ref[0] * own + ones for own, _, ones in map_masks]

    lp = lam_ref[...]
    lam = (jnp.exp(jnp.sum(lp[0:1] * lp[1:2], axis=1, keepdims=True))
           - jnp.exp(jnp.sum(lp[2:3] * lp[3:4], axis=1, keepdims=True)) + lambda_init)
    r = lax.broadcasted_iota(jnp.int32, (LANES, LANES), 0) // HEAD_DIM
    cc = lax.broadcasted_iota(jnp.int32, (LANES, LANES), 1) // HEAD_DIM
    same = jnp.where(r == cc, 1.0, 0.0).astype(BF16)

    def finish(accs):
        o1 = _normalize_pair(accs[0], accs[2], lane_q)
        o2 = _normalize_pair(accs[1], accs[3], lane_q)
        o = o1 - lam * o2
        o2_hi, o2_lo = _split2(o * o)
        ms = (_dot(o2_hi, same) + _dot(o2_lo, same)) * (1.0 / HEAD_DIM)
        y = o * lax.rsqrt(ms + RMS_EPS) * g_ref[...]
        o_ref[0] = (y * out_scale).astype(o_ref.dtype)

    _two_pass_attention(qa, ka_ref, va_ref, s_ref, (0, 0, 1, 1), i, tq, finish)


def _diff_attn(proj, bias, lam_params, g2, tq, out_scale, lambda_init):
    b, s, _ = proj.shape
    nq = s // tq
    qc, kc, vc = OFF_QD // LANES, OFF_KD // LANES, OFF_VD // LANES
    return pl.pallas_call(
        functools.partial(_diff_kernel, tq=tq, out_scale=out_scale, lambda_init=lambda_init),
        grid=(b, 2, nq),
        in_specs=[
            pl.BlockSpec((1, tq, LANES), lambda bi, p, i: (bi, i, qc + p)),
            pl.BlockSpec((1, s, LANES), lambda bi, p, i: (bi, 0, kc + p)),
            pl.BlockSpec((1, s, LANES), lambda bi, p, i: (bi, 0, vc + p)),
            pl.BlockSpec((2, 2, s, LANES), lambda bi, p, i: (p, 0, 0, 0)),
            pl.BlockSpec((4, DIFF_QK_DIM), lambda bi, p, i: (0, 0)),
            pl.BlockSpec((1, LANES), lambda bi, p, i: (0, 0)),
        ],
        out_specs=pl.BlockSpec((1, tq, LANES), lambda bi, p, i: (bi, i, p)),
        out_shape=jax.ShapeDtypeStruct((b, s, BW), BF16),
        scratch_shapes=[
            pltpu.VMEM((4, s, LANES), BF16),
            pltpu.VMEM((2, s, LANES), BF16),
            pltpu.VMEM((4, nq, tq, tq), F32),
        ],
        compiler_params=_cparams(("parallel", "parallel", "arbitrary")),
        name="diff_attn",
    )(proj, proj, proj, bias, lam_params, g2)


def _dil_masks():
    t = np.arange(DB)[:, None]
    s = np.arange(DB)[None, :]
    first = np.where(s <= t, 0.0, NEG)
    s2 = np.arange(2 * DB)[None, :]
    dist = t + DB - s2
    band = np.where((dist >= 0) & (dist <= DB), 0.0, NEG)
    return first.astype(np.float32), band.astype(np.float32)


def _dil_bias_table(seq, slopes):
    out = np.zeros((N_GROUPS, N_HEADS, seq, LANES), np.float32)
    rho = np.arange(seq)
    for g, (window, dil) in enumerate(DIL_PATTERNS):
        cls_len = seq // dil
        pos = (rho % cls_len) * dil + rho // cls_len
        for hh in range(N_HEADS):
            spare = HEAD_DIM if hh % 2 == 0 else 0
            rest = pos.astype(np.float64) * (slopes[g][hh] * LOG2E)
            for i in range(N_PIECES):
                piece = rest.astype(ml_dtypes.bfloat16).astype(np.float64)
                out[g, hh, :, spare + i] = piece
                rest = rest - piece
    return out.astype(ml_dtypes.bfloat16)


def _dil_kernel(*refs, slopes):
    q_refs, k_refs, v_refs = refs[0:3], refs[3:6], refs[6:9]
    (bias_ref, first_ref, band_ref, o_ref, qa_ref, ka_ref, va_ref, og_ref, lse_ref,
     s_ref, m_ref, acc_ref) = refs[9:]
    p = pl.program_id(0)
    s_len = k_refs[0].shape[1]
    n_blk = s_len // DB

    def slope_of(g, h):
        return jnp.where(p == 0, slopes[g][h], slopes[g][2 + h]).astype(F32)

    lane1 = lax.broadcasted_iota(jnp.int32, (1, LANES), 1)
    for h in range(2):
        own_l = (lane1 < HEAD_DIM) if h == 0 else (lane1 >= HEAD_DIM)
        spare = HEAD_DIM if h == 0 else 0
        own = jnp.where(own_l, 1.0, 0.0).astype(BF16)
        other = jnp.where(own_l, 0.0, 1.0).astype(BF16)
        ones = _ones_at(lane1, spare, N_PIECES).astype(BF16)
        for g in range(N_GROUPS):
            qa_ref[2 * g + h] = q_refs[g][0] * own + ones
            ka_ref[2 * g + h] = k_refs[g][0] * own + bias_ref[g, h]
            va_ref[2 * g + h] = v_refs[g][0] * own + other

    low = lax.broadcasted_iota(jnp.int32, (s_len, LANES), 1) < HEAD_DIM
    for g, (window, dil) in enumerate(DIL_PATTERNS):
        cls_blocks = n_blk // dil
        cls_len = s_len // dil

        def window_of(j):
            first = j % cls_blocks == 0
            return (j * DB if first else (j - 1) * DB), (DB if first else 2 * DB), first

        for j in range(n_blk):
            k0, kw, first = window_of(j)
            mask = first_ref[...] if first else band_ref[...]
            for h in range(2):
                gh = 2 * g + h
                s = _dot_nt(qa_ref[gh, j * DB:(j + 1) * DB, :], ka_ref[gh, k0:k0 + kw, :]) + mask
                s_ref[h, j, :, :kw] = s
                m = jnp.max(s, axis=1, keepdims=True)
                m_ref[h, j * DB:(j + 1) * DB, :] = jnp.broadcast_to(m, (DB, LANES))
        for j in range(n_blk):
            k0, kw, first = window_of(j)
            for h in range(2):
                gh = 2 * g + h
                m = m_ref[h, j * DB:(j + 1) * DB, :]
                mb = m if first else jnp.concatenate([m, m], axis=1)
                pr = jnp.exp2(s_ref[h, j, :, :kw] - mb).astype(BF16)
                acc_ref[h, j * DB:(j + 1) * DB, :] = _dot(pr, va_ref[gh, k0:k0 + kw, :])
        acc0, acc1 = acc_ref[0], acc_ref[1]
        num = jnp.where(low, acc0, acc1)
        den = jnp.where(low, pltpu.roll(acc0, HEAD_DIM, 1), pltpu.roll(acc1, HEAD_DIM, 1))
        out_g = num / den
        lse_g = jnp.where(low, m_ref[0], m_ref[1]) + jnp.log(den) * LOG2E
        for c in range(dil):
            rows = pl.ds(c, cls_len, stride=dil) if dil > 1 else pl.ds(0, cls_len)
            og_ref[g, rows, :] = out_g[c * cls_len:(c + 1) * cls_len]
            lse_ref[g, rows, :] = lse_g[c * cls_len:(c + 1) * cls_len]

    lane_s = lax.broadcasted_iota(jnp.int32, (s_len, LANES), 1)
    t = lax.broadcasted_iota(jnp.int32, (s_len, LANES), 0).astype(F32)
    lses = []
    for g in range(N_GROUPS):
        slope_lane = jnp.where(lane_s < HEAD_DIM, slope_of(g, 0), slope_of(g, 1)) * LOG2E
        lses.append(lse_ref[g] - slope_lane * t)
    top = jnp.maximum(jnp.maximum(lses[0], lses[1]), lses[2])
    wsum = jnp.zeros((s_len, LANES), F32)
    out = jnp.zeros((s_len, LANES), F32)
    for g in range(N_GROUPS):
        w = jnp.exp2(lses[g] - top)
        wsum = wsum + w
        out = out + w * og_ref[g]
    o_ref[0] = (out / wsum).astype(o_ref.dtype)


def _dil_attn(proj, g1, g2, bias, first, band, slopes):
    b, s, _ = proj.shape
    per = BW // LANES

    def spec(blk):
        return pl.BlockSpec((1, s, LANES), lambda p, bi: (bi, 0, blk + p))

    arrays, in_specs = [], []
    for kind, off in enumerate((OFF_QB, OFF_KB, OFF_VB)):
        arrays += [proj, g1, g2]
        in_specs += [spec(off // LANES), spec(kind * per), spec(kind * per)]
    in_specs += [pl.BlockSpec((N_GROUPS, 2, s, LANES), lambda p, bi: (0, p, 0, 0)),
                 pl.BlockSpec(first.shape, lambda p, bi: (0, 0)),
                 pl.BlockSpec(band.shape, lambda p, bi: (0, 0))]
    return pl.pallas_call(
        functools.partial(_dil_kernel, slopes=slopes),
        grid=(2, b),
        in_specs=in_specs,
        out_specs=pl.BlockSpec((1, s, LANES), lambda p, bi: (bi, 0, p)),
        out_shape=jax.ShapeDtypeStruct((b, s, BW), BF16),
        scratch_shapes=[
            pltpu.VMEM((2 * N_GROUPS, s, LANES), BF16),
            pltpu.VMEM((2 * N_GROUPS, s, LANES), BF16),
            pltpu.VMEM((2 * N_GROUPS, s, LANES), BF16),
            pltpu.VMEM((N_GROUPS, s, LANES), F32),
            pltpu.VMEM((N_GROUPS, s, LANES), F32),
            pltpu.VMEM((2, s // DB, DB, 2 * DB), F32),
            pltpu.VMEM((2, s, LANES), F32),
            pltpu.VMEM((2, s, LANES), F32),
        ],
        compiler_params=_cparams(("parallel", "parallel")),
        name="dil_attn",
    )(*arrays, bias, first, band)


def _sb_kernel(q_ref, k_ref, v_ref, o_ref, *, tq, tk):
    i = pl.program_id(2)
    n_diag = tq // tk
    lane_q = lax.broadcasted_iota(jnp.int32, (tq, LANES), 1)
    q = q_ref[0].astype(F32)
    row = lax.broadcasted_iota(jnp.int32, (tq, tk), 0)
    col = lax.broadcasted_iota(jnp.int32, (tq, tk), 1)
    tr = lax.broadcasted_iota(jnp.int32, (tk, tk), 0)
    tc = lax.broadcasted_iota(jnp.int32, (tk, tk), 1)
    tri = jnp.where(tr > tc, 1.0, 0.0).astype(BF16)
    qn = [jnp.where((lane_q < HEAD_DIM) if h == 0 else (lane_q >= HEAD_DIM), -q, 0.0).astype(BF16)
          for h in range(2)]

    def scores(h, c, strict):
        kc = k_ref[0, c * tk:(c + 1) * tk, :]
        zn = _dot_nt(qn[h], kc)
        lk = jnp.minimum(zn, 0.0) - jnp.log(1.0 + jnp.exp(-jnp.abs(zn)))
        if strict is not None:
            lk = jnp.where(strict, lk, 0.0)
        lkb = lk.astype(BF16)
        suffix = _dot(lkb, tri)
        total = suffix[:, 0:1] + lkb[:, 0:1].astype(F32)
        return lk - zn + suffix, total

    def weigh(h, c, strict, logit, total, run, acc):
        vc = v_ref[0, c * tk:(c + 1) * tk, :]
        run_b = jnp.concatenate([run] * (tk // LANES), axis=1)
        a = jnp.exp(logit + run_b)
        if strict is not None:
            a = jnp.where(strict, a, 0.0)
        acc = acc + _dot(a.astype(BF16), vc)
        return run + jnp.broadcast_to(total, (tq, LANES)), acc

    def step(chunks, stricts, carry):
        parts = [[scores(h, c, st) for c, st in zip(chunks, stricts)] for h in range(2)]
        out = []
        for h in range(2):
            run, acc = carry[2 * h], carry[2 * h + 1]
            for c, st, (logit, total) in zip(chunks, stricts, parts[h]):
                run, acc = weigh(h, c, st, logit, total, run, acc)
            out += [run, acc]
        return tuple(out)

    def block(k):
        carry = tuple(jnp.zeros((tq, LANES), F32) for _ in range(4))
        diag = list(range(n_diag - 1, -1, -1))
        carry = step([k * n_diag + d for d in diag], [col + d * tk < row for d in diag], carry)
        for n in range(k):
            newest = k * n_diag - 1 - n * n_diag
            carry = step([newest - d for d in range(n_diag)], [None] * n_diag, carry)
        o_ref[0] = jnp.where(lane_q < HEAD_DIM, carry[1], carry[3]).astype(o_ref.dtype)

    for k in range(k_ref.shape[1] // tq):
        pl.when(i == k)(functools.partial(block, k))


def _sb_attn(proj, tq, tk):
    b, s, _ = proj.shape
    nq = s // tq
    qc, kc, vc = OFF_QA // LANES, OFF_KA // LANES, OFF_VA // LANES
    return pl.pallas_call(
        functools.partial(_sb_kernel, tq=tq, tk=tk),
        grid=(b, 2, nq),
        in_specs=[
            pl.BlockSpec((1, tq, LANES), lambda bi, p, i: (bi, i, qc + p)),
            pl.BlockSpec((1, s, LANES), lambda bi, p, i: (bi, 0, kc + p)),
            pl.BlockSpec((1, s, LANES), lambda bi, p, i: (bi, 0, vc + p)),
        ],
        out_specs=pl.BlockSpec((1, tq, LANES), lambda bi, p, i: (bi, i, p)),
        out_shape=jax.ShapeDtypeStruct((b, s, BW), BF16),
        compiler_params=_cparams(("parallel", "parallel", "arbitrary")),
        name="sb_attn",
    )(proj, proj, proj)


def _merge_kernel(x_ref, g_ref, ya_ref, yb_ref, yc_ref, yd_ref, wg_ref, wb_ref, wo_ref, o_ref):
    x = x_ref[...]
    h = _rms(x, g_ref[...]).astype(BF16)
    merged = None
    for n, y_ref in enumerate((ya_ref, yb_ref, yc_ref, yd_ref)):
        br = _dot(y_ref[...], wb_ref[n])
        gl = _dot(h, wg_ref[:, n * D_MODEL:(n + 1) * D_MODEL])
        term = br * (0.5 + 0.5 * jnp.tanh(0.5 * gl))
        merged = term if merged is None else merged + term
    o_ref[...] = x + _dot(merged.astype(BF16), wo_ref[...])


def _merge_out(x2d, g, ys, wg, wb, wo, layer, tm):
    n, d = x2d.shape
    resident = pl.Buffered(1)
    return pl.pallas_call(
        _merge_kernel,
        grid=(n // tm,),
        in_specs=[pl.BlockSpec((tm, d), lambda i: (i, 0)),
                  pl.BlockSpec((1, d), lambda i: (0, 0))]
        + [pl.BlockSpec((tm, BW), lambda i: (i, 0))] * 4
        + [
            pl.BlockSpec((None, d, GATE_W), lambda i: (layer, 0, 0), pipeline_mode=resident),
            pl.BlockSpec((None, 4, BW, d), lambda i: (layer, 0, 0, 0), pipeline_mode=resident),
            pl.BlockSpec((None, d, d), lambda i: (layer, 0, 0), pipeline_mode=resident),
        ],
        out_specs=pl.BlockSpec((tm, d), lambda i: (i, 0)),
        out_shape=jax.ShapeDtypeStruct((n, d), F32),
        compiler_params=_cparams(("parallel",)),
        name="merge_out",
    )(x2d, g, *ys, wg, wb, wo)


def _mlp_kernel(x_ref, g_ref, wu_ref, wd_ref, gf_ref, o_ref, *, final_norm, tf):
    x = x_ref[...]
    h = _rms(x, g_ref[...]).astype(BF16)
    y = x
    for c in range(wu_ref.shape[1] // tf):
        up = _dot(h, wu_ref[:, c * tf:(c + 1) * tf])
        act = jnp.square(jnp.maximum(up, 0.0)).astype(BF16)
        y = y + _dot(act, wd_ref[c * tf:(c + 1) * tf, :])
    if final_norm:
        y = _rms(y, gf_ref[...])
    o_ref[...] = y


def _mlp(x2d, g, wu, wd, layer, gf, tm, tf, final_norm):
    n, d = x2d.shape
    dff = wu.shape[2]
    resident = pl.Buffered(1)
    return pl.pallas_call(
        functools.partial(_mlp_kernel, final_norm=final_norm, tf=tf),
        grid=(n // tm,),
        in_specs=[
            pl.BlockSpec((tm, d), lambda i: (i, 0)),
            pl.BlockSpec((1, d), lambda i: (0, 0)),
            pl.BlockSpec((None, d, dff), lambda i: (layer, 0, 0), pipeline_mode=resident),
            pl.BlockSpec((None, dff, d), lambda i: (layer, 0, 0), pipeline_mode=resident),
            pl.BlockSpec((1, d), lambda i: (0, 0)),
        ],
        out_specs=pl.BlockSpec((tm, d), lambda i: (i, 0)),
        out_shape=jax.ShapeDtypeStruct((n, d), F32),
        compiler_params=_cparams(("parallel",)),
        name="mlp",
    )(x2d, g, wu, wd, gf)


def _col_scale():
    cs = np.ones((1, W_MAIN), np.float32)
    cs[0, OFF_QA:OFF_QA + BW] = HEAD_DIM ** -0.5
    cs[0, OFF_QB:OFF_QB + N_GROUPS * BW] = HEAD_DIM ** -0.5 * LOG2E
    cs[0, OFF_QC:OFF_QC + BW] = HEAD_DIM ** -0.5 * LOG2E
    cs[0, OFF_QD:OFF_QD + BW] = DIFF_QK_DIM ** -0.5 * LOG2E
    return cs


def kernel(x, mix_norm_g, w_in, b_forget, lambda_q1, lambda_k1, lambda_q2, lambda_k2,
           diff_norm_g, w_branch, w_out, mlp_norm_g, w_up, w_down, final_norm_g):
    bsz, seq, d = x.shape
    depth = w_in.shape[0]
    n = bsz * seq
    tq_big = min(512, seq)
    tm = min(1024, n)

    slopes = [float(v) for v in _alibi_slopes()]
    n0, n1 = N_HEADS, 2 * N_HEADS
    slopes_dil = (tuple(slopes[:n0]), tuple(slopes[n1:n1 + N_HEADS]), tuple(slopes[n1 + N_HEADS:]))
    slopes_diff = tuple(slopes[n0:n1])

    assert all(w // r == DB and seq % (DB * r) == 0 for w, r in DIL_PATTERNS) and n % tm == 0
    first, band = (jnp.asarray(m) for m in _dil_masks())
    dil_bias = jnp.asarray(_dil_bias_table(seq, slopes_dil))
    diff_bias = jnp.asarray(_diff_bias_table(seq, slopes_diff))
    fox_place = jnp.asarray(_fox_placement(), BF16)
    tril = jnp.asarray(np.tril(np.ones((2 * LANES, 2 * LANES), np.float32)), BF16)
    colscale = jnp.asarray(_col_scale())

    d_end = F_COL + N_HEADS + 3 * BW
    w_cols = (w_in.reshape(depth, d // LANES, LANES, w_in.shape[2]).transpose(3, 1, 0, 2)
              .reshape(w_in.shape[2], d // LANES * depth, LANES))
    w_a = _w_in_section(w_cols, depth, 0, F_COL, 256)
    w_d = _w_in_section(w_cols, depth, F_COL + N_HEADS, 3 * BW, 256)
    w_g = _w_in_section(w_cols, depth, d_end, GATE_W, 256)
    wf = jnp.pad(w_in[:, :, F_COL:F_COL + N_HEADS], ((0, 0), (0, 0), (0, LANES - N_HEADS)))
    wf_hi = wf.astype(BF16)
    wf_lo = (wf - wf_hi.astype(F32)).astype(BF16)
    w_up_bf = _cast_bf16(w_up.reshape(depth * d, D_FF), 256).reshape(depth, d, D_FF)
    w_down_bf = _cast_bf16(w_down.reshape(depth * D_FF, d), 1024).reshape(depth, D_FF, d)
    w_branch_bf = _cast_bf16(w_branch.reshape(depth * 4 * BW, d), 1024).reshape(depth, 4, BW, d)
    w_out_bf = _cast_bf16(w_out.reshape(depth * d, d), 1024).reshape(depth, d, d)

    x2d = x.reshape(n, d)
    for l in range(depth):
        lambda_init = 0.8 - 0.6 * math.exp(-0.3 * l)
        bf = jnp.pad(b_forget[l].astype(F32), (0, LANES - N_HEADS)).reshape(1, LANES)
        g_mix = mix_norm_g[l].reshape(1, d)

        proj, pg1, pg2, logf = _norm_proj(x2d, g_mix, w_a, w_d, wf_hi, wf_lo, l, colscale, bf, bsz,
                                          min(1024, seq))
        fox_bias = _forget_cum(logf, tril, fox_place, bsz)
        proj3 = proj.reshape(bsz, seq, W_MAIN)

        y_a = _sb_attn(proj3, tq_big, min(256, seq))
        y_b = _dil_attn(proj3, pg1.reshape(bsz, seq, -1), pg2.reshape(bsz, seq, -1), dil_bias,
                        first, band, slopes_dil)
        y_c = _fox_attn(proj3, fox_bias, tq_big)
        lam_params = jnp.stack([lambda_q1[l], lambda_k1[l], lambda_q2[l], lambda_k2[l]]).astype(F32)
        g2 = jnp.tile(diff_norm_g[l].astype(F32), 2).reshape(1, LANES)
        y_d = _diff_attn(proj3, diff_bias, lam_params, g2, tq_big, 1.0 - lambda_init, lambda_init)

        ys = [y.reshape(n, BW) for y in (y_a, y_b, y_c, y_d)]
        x2d = _merge_out(x2d, g_mix, ys, w_g, w_branch_bf, w_out_bf, l, tm)
        x2d = _mlp(x2d, mlp_norm_g[l].reshape(1, d), w_up_bf, w_down_bf, l,
                   final_norm_g.reshape(1, d), tm, 1024,
                   final_norm=(l == depth - 1))
    return x2d.reshape(bsz, seq, d)
```
